```python
import jax, jax.numpy as jnp
from jax import lax
import numpy as np

D_MODEL = 1024
BATCH = 16
SEQ = 4096
DEPTH = 1

D_MIX = D_MODEL
GLA_HEADS = 4
GLA_DK = 64
GLA_DV = 128
GLA_RANK = 16
GLA_GATE_NORM = 16.0
GLA_CHUNK = 64
SB_HEADS = 8
SB_DH = 64
SB_BLOCK = 128
D_FF = 2816
EPS = 1e-6

GLA_QK = GLA_HEADS * GLA_DK
GLA_V = GLA_HEADS * GLA_DV
SB_W = SB_HEADS * SB_DH
PROJ_SIZES = (GLA_QK, GLA_QK, GLA_V, GLA_V, GLA_RANK, SB_W, SB_W, SB_W)
PROJ_SPLITS = tuple(int(s) for s in np.cumsum(PROJ_SIZES)[:-1])
D_IN = int(sum(PROJ_SIZES))

kernel_name = "hymba_style_gla_stickbreaking_macaron"


def rmsnorm(x, g):
    xf = x.astype(jnp.float32)
    y = xf * lax.rsqrt(jnp.mean(xf * xf, axis=-1, keepdims=True) + EPS)
    return (y * g.astype(jnp.float32)).astype(x.dtype)


def swiglu(h, w_gate, w_up, w_down):
    return (jax.nn.silu(h @ w_gate) * (h @ w_up)) @ w_down


def gla_chunked(q, k, v, log_a):
    B, T, H, DK = q.shape
    DV = v.shape[-1]
    C = GLA_CHUNK
    N = T // C

    def chunks(t):
        return t.reshape(B, N, C, H, t.shape[-1]).transpose(0, 3, 1, 2, 4)

    q, k, v, log_a = chunks(q * (GLA_DK ** -0.5)), chunks(k), chunks(v), chunks(log_a)
    b = jnp.cumsum(log_a, axis=3)
    b_last = b[:, :, :, -1:, :]
    q_dec = q * jnp.exp(b)
    k_inv = k * jnp.exp(-b)
    k_end = k * jnp.exp(b_last - b)
    causal = jnp.tril(jnp.ones((C, C), dtype=bool))
    scores = jnp.where(causal, jnp.einsum('bhnid,bhnjd->bhnij', q_dec, k_inv), 0.0)
    o_intra = jnp.einsum('bhnij,bhnjv->bhniv', scores, v)

    def step(state, xs):
        k_n, v_n, decay_n = xs
        new = decay_n[..., None] * state + jnp.einsum('bhcd,bhcv->bhdv', k_n, v_n)
        return new, state

    xs = (jnp.moveaxis(k_end, 2, 0), jnp.moveaxis(v, 2, 0),
          jnp.moveaxis(jnp.exp(b_last[:, :, :, 0, :]), 2, 0))
    _, states = lax.scan(step, jnp.zeros((B, H, DK, DV), jnp.float32), xs)
    states = jnp.moveaxis(states, 0, 2)
    o_inter = jnp.einsum('bhnid,bhndv->bhniv', q_dec, states)
    o = o_intra + o_inter
    return o.transpose(0, 2, 3, 1, 4).reshape(B, T, H, DV)


def stick_breaking(q, k, v):
    B, T, H, D = q.shape
    NB = T // SB_BLOCK
    scale = D ** -0.5
    q = q.transpose(0, 2, 1, 3)
    k = k.transpose(0, 2, 1, 3)
    v = v.transpose(0, 2, 1, 3)
    q_blocks = q.reshape(B, H, NB, SB_BLOCK, D).transpose(2, 0, 1, 3, 4)
    key_pos = jnp.arange(T)

    def block(args):
        qb, i = args
        z = jnp.einsum('bhqd,bhkd->bhqk', qb, k) * scale
        t_pos = i * SB_BLOCK + jnp.arange(SB_BLOCK)
        mask = key_pos[None, :] < t_pos[:, None]
        log_keep = jnp.where(mask, jax.nn.log_sigmoid(-z), 0.0)
        between = lax.cumsum(log_keep, axis=3, reverse=True) - log_keep
        w = jnp.where(mask, jnp.exp(jax.nn.log_sigmoid(z) + between), 0.0)
        return jnp.einsum('bhqk,bhkd->bhqd', w, v)

    out = lax.map(block, (q_blocks, jnp.arange(NB)))
    return out.transpose(1, 0, 3, 2, 4).reshape(B, T, H, D)


def setup_inputs(seed: int = 0) -> dict:
    key = jax.random.key(seed)
    ks = jax.random.split(key, 20)
    f32 = jnp.float32

    def w(k, shape, fan_in):
        return jax.random.normal(k, shape, f32) * (fan_in ** -0.5)

    def gain(k, n):
        return 1.0 + 0.02 * jax.random.normal(k, (DEPTH, n), f32)

    return {
        "x": jax.random.normal(ks[0], (BATCH, SEQ, D_MODEL), f32),
        "ffn1_norm": gain(ks[1], D_MODEL),
        "ffn1_w_gate": w(ks[2], (DEPTH, D_MODEL, D_FF), D_MODEL),
        "ffn1_w_up": w(ks[3], (DEPTH, D_MODEL, D_FF), D_MODEL),
        "ffn1_w_down": w(ks[4], (DEPTH, D_FF, D_MODEL), D_FF),
        "mix_norm": gain(ks[5], D_MODEL),
        "w_in": w(ks[6], (DEPTH, D_MODEL, D_IN), D_MODEL),
        "w_gk_up": w(ks[7], (DEPTH, GLA_RANK, GLA_QK), GLA_RANK),
        "b_gk": 0.1 * jax.random.normal(ks[8], (DEPTH, GLA_QK), f32),
        "gla_out_norm": gain(ks[9], GLA_DV),
        "sb_q_norm": gain(ks[10], SB_DH),
        "sb_k_norm": gain(ks[11], SB_DH),
        "w_out": w(ks[12], (DEPTH, D_MIX, D_MODEL), D_MIX),
        "ffn2_norm": gain(ks[13], D_MODEL),
        "ffn2_w_gate": w(ks[14], (DEPTH, D_MODEL, D_FF), D_MODEL),
        "ffn2_w_up": w(ks[15], (DEPTH, D_MODEL, D_FF), D_MODEL),
        "ffn2_w_down": w(ks[16], (DEPTH, D_FF, D_MODEL), D_FF),
    }


def reference(x, ffn1_norm, ffn1_w_gate, ffn1_w_up, ffn1_w_down, mix_norm, w_in,
              w_gk_up, b_gk, gla_out_norm, sb_q_norm, sb_k_norm, w_out,
              ffn2_norm, ffn2_w_gate, ffn2_w_up, ffn2_w_down):
    B, T, _ = x.shape
    f32 = jnp.float32
    for l in range(DEPTH):
        x = x + 0.5 * swiglu(rmsnorm(x, ffn1_norm[l]), ffn1_w_gate[l], ffn1_w_up[l], ffn1_w_down[l])

        h = rmsnorm(x, mix_norm[l])
        p = h @ w_in[l]
        g_q, g_k, g_v, g_gate, g_lr, s_q, s_k, s_v = jnp.split(p, PROJ_SPLITS, axis=-1)

        log_a = jax.nn.log_sigmoid((g_lr @ w_gk_up[l] + b_gk[l]).astype(f32)) / GLA_GATE_NORM
        o_gla = gla_chunked(g_q.astype(f32).reshape(B, T, GLA_HEADS, GLA_DK),
                            g_k.astype(f32).reshape(B, T, GLA_HEADS, GLA_DK),
                            g_v.astype(f32).reshape(B, T, GLA_HEADS, GLA_DV),
                            log_a.reshape(B, T, GLA_HEADS, GLA_DK))
        o_gla = rmsnorm(o_gla, gla_out_norm[l]) * jax.nn.silu(
            g_gate.astype(f32).reshape(B, T, GLA_HEADS, GLA_DV))
        o_gla = o_gla.reshape(B, T, GLA_V)

        q_sb = rmsnorm(s_q.astype(f32).reshape(B, T, SB_HEADS, SB_DH), sb_q_norm[l])
        k_sb = rmsnorm(s_k.astype(f32).reshape(B, T, SB_HEADS, SB_DH), sb_k_norm[l])
        o_sb = stick_breaking(q_sb, k_sb, s_v.astype(f32).reshape(B, T, SB_HEADS, SB_DH))
        o_sb = o_sb.reshape(B, T, SB_W)

        mixed = jnp.concatenate([o_gla, o_sb], axis=-1).astype(x.dtype)
        x = x + mixed @ w_out[l]

        x = x + 0.5 * swiglu(rmsnorm(x, ffn2_norm[l]), ffn2_w_gate[l], ffn2_w_up[l], ffn2_w_down[l])
    return x
```

```python
import functools

import jax
import jax.numpy as jnp
import numpy as np
from jax import lax
from jax.experimental import pallas as pl
from jax.experimental.pallas import tpu as pltpu

EPS = 1e-6
GLA_HEADS = 4
GLA_DK = 64
GLA_DV = 128
GLA_RANK = 16
GLA_GATE_NORM = 16.0
GLA_CHUNK = 64
SB_HEADS = 8
SB_DH = 64
GLA_QK = GLA_HEADS * GLA_DK
GLA_V = GLA_HEADS * GLA_DV
SB_W = SB_HEADS * SB_DH

LANES = 128
RANK_PAD = LANES
VMEM_LIMIT_BYTES = 56 * 1024 * 1024

TOKEN_TILE = 512
SB_TILE = 256
FFN_CHUNKS = 2

F32 = jnp.float32
BF16 = jnp.bfloat16

_C_GQ = 0
_C_GK = _C_GQ + GLA_QK
_C_GV = _C_GK + GLA_QK
_C_GG = _C_GV + GLA_V
_C_SQ = _C_GG + GLA_V
_C_SK = _C_SQ + SB_W
_C_SV = _C_SK + SB_W
_C_LR = _C_SV + SB_W
_C_END = _C_LR + RANK_PAD


def _dot(a, b):
    return jnp.dot(a, b, preferred_element_type=F32)


def _dot_nt(a, b):
    return lax.dot_general(a, b, (((1,), (1,)), ((), ())), preferred_element_type=F32)


def _dot_tn(a, b):
    return lax.dot_general(a, b, (((0,), (0,)), ((), ())), preferred_element_type=F32)


def _rms(x, g):
    return x * lax.rsqrt(jnp.mean(x * x, axis=-1, keepdims=True) + EPS) * g


def _softplus(z):
    return jnp.maximum(z, 0.0) + jnp.log(1.0 + jnp.exp(-jnp.abs(z)))


def _split_bf16(x):
    hi = x.astype(BF16)
    lo = (x - hi.astype(F32)).astype(BF16)
    return hi, lo


def _swiglu_half_step(x, g_ref, wg_ref, wu_ref, wd_ref):
    h = _rms(x, g_ref[...]).astype(BF16)
    d_ff = wg_ref.shape[1]
    step = d_ff // FFN_CHUNKS
    acc = jnp.zeros(x.shape, F32)
    for c in range(FFN_CHUNKS):
        lo, hi = c * step, (c + 1) * step
        gate = _dot(h, wg_ref[:, lo:hi])
        up = _dot(h, wu_ref[:, lo:hi])
        act = (gate * jax.nn.sigmoid(gate) * up).astype(BF16)
        acc = acc + _dot(act, wd_ref[lo:hi, :])
    return x + 0.5 * acc


def _ffn_proj_kernel(x_ref, g1_ref, wg_ref, wu_ref, wd_ref, g2_ref, wp_ref, wup_ref, bgk_ref,
                     sqg_ref, skg_ref, ones_ref,
                     x1_ref, gq_ref, gk_ref, la_ref, gv_ref, gg_ref, sqt_ref, sk_ref, svt_ref):
    x1 = _swiglu_half_step(x_ref[...], g1_ref, wg_ref, wu_ref, wd_ref)
    x1_ref[...] = x1
    h = _rms(x1, g2_ref[...]).astype(BF16)

    def proj(lo, hi):
        return _dot(h, wp_ref[:, lo:hi])

    gq = proj(_C_GQ, _C_GK) * (GLA_DK ** -0.5)
    gk = proj(_C_GK, _C_GV)
    lr = proj(_C_LR, _C_END).astype(BF16)
    pre = _dot(lr, wup_ref[...]) + bgk_ref[...]
    la = -_softplus(-pre) * (1.0 / GLA_GATE_NORM)
    for hd in range(GLA_HEADS):
        sl = slice(hd * GLA_DK, (hd + 1) * GLA_DK)
        gq_ref[hd] = gq[:, sl]
        gk_ref[hd] = gk[:, sl]
        la_ref[hd] = la[:, sl]
    gv_ref[...] = proj(_C_GV, _C_GG).astype(BF16)
    gg_ref[...] = proj(_C_GG, _C_SQ)

    def head_norm(p, g_ref):
        ms = _dot((p * p).astype(BF16), ones_ref[...]) * (1.0 / SB_DH)
        return p * lax.rsqrt(ms + EPS) * g_ref[...]

    n_sub = sqt_ref.shape[0]
    sq = head_norm(proj(_C_SQ, _C_SK), sqg_ref) * (SB_DH ** -0.5)
    sqt = sq.T.astype(BF16)
    sk_ref[...] = head_norm(proj(_C_SK, _C_SV), skg_ref).astype(BF16)
    svt = proj(_C_SV, _C_LR).T.astype(BF16)
    for c in range(n_sub):
        sqt_ref[c] = sqt[:, c * SB_TILE:(c + 1) * SB_TILE]
        svt_ref[c] = svt[:, c * SB_TILE:(c + 1) * SB_TILE]


def _const_spec(shape):
    nd = len(shape)
    return pl.BlockSpec(shape, lambda *_: (0,) * nd, pipeline_mode=pl.Buffered(1))


def _ffn_proj(x, g1, wg, wu, wd, g2, wp, wup, bgk, sqg, skg, ones_bd, tm):
    B, T, D = x.shape
    nt = T // tm
    n_sub = tm // SB_TILE
    nblk = T // SB_TILE
    tok = lambda w: pl.BlockSpec((None, tm, w), lambda b, i: (b, i, 0))
    headmajor = pl.BlockSpec((None, GLA_HEADS, tm, GLA_DK), lambda b, i: (b, 0, i, 0))
    transposed = pl.BlockSpec((None, n_sub, SB_W, SB_TILE), lambda b, i: (b, i, 0, 0))
    consts = (g1, wg, wu, wd, g2, wp, wup, bgk, sqg, skg, ones_bd)
    return pl.pallas_call(
        _ffn_proj_kernel,
        grid=(B, nt),
        in_specs=[tok(D)] + [_const_spec(c.shape) for c in consts],
        out_specs=[tok(D), headmajor, headmajor, headmajor, tok(GLA_V), tok(GLA_V),
                   transposed, tok(SB_W), transposed],
        out_shape=[
            jax.ShapeDtypeStruct((B, T, D), F32),
            jax.ShapeDtypeStruct((B, GLA_HEADS, T, GLA_DK), F32),
            jax.ShapeDtypeStruct((B, GLA_HEADS, T, GLA_DK), F32),
            jax.ShapeDtypeStruct((B, GLA_HEADS, T, GLA_DK), F32),
            jax.ShapeDtypeStruct((B, T, GLA_V), BF16),
            jax.ShapeDtypeStruct((B, T, GLA_V), F32),
            jax.ShapeDtypeStruct((B, nblk, SB_W, SB_TILE), BF16),
            jax.ShapeDtypeStruct((B, T, SB_W), BF16),
            jax.ShapeDtypeStruct((B, nblk, SB_W, SB_TILE), BF16),
        ],
        compiler_params=pltpu.CompilerParams(
            dimension_semantics=("parallel", "parallel"), vmem_limit_bytes=VMEM_LIMIT_BYTES),
        name="ffn1_proj",
    )(x, *consts)


def _gla_kernel(q_ref, k_ref, la_ref, v_ref, gate_ref, g_ref, tri_ref, o_ref):
    C = GLA_CHUNK
    n_chunks = q_ref.shape[0] // C
    row = lax.broadcasted_iota(jnp.int32, (C, C), 0)
    col = lax.broadcasted_iota(jnp.int32, (C, C), 1)
    causal = col <= row
    tri2 = tri_ref[...]
    gain = g_ref[...]

    def chunk(n, state_t):
        r = pl.ds(pl.multiple_of(n * C, C), C)
        q = q_ref[r, :]
        k = k_ref[r, :]
        v = v_ref[r, :]
        la_hi, la_lo = _split_bf16(la_ref[r, :])
        b = _dot(tri2, jnp.concatenate([la_hi, la_lo], axis=0))
        b_last = b[C - 1:C, :]
        q_dec = (q * jnp.exp(b)).astype(BF16)
        k_inv = (k * jnp.exp(-b)).astype(BF16)
        k_end = (k * jnp.exp(b_last - b)).astype(BF16)
        scores = jnp.where(causal, _dot_nt(q_dec, k_inv), 0.0).astype(BF16)
        o = _dot(scores, v) + _dot_nt(q_dec, state_t.astype(BF16))
        new_state_t = state_t * jnp.exp(b_last) + _dot_tn(v, k_end)
        o = _rms(o, gain)
        gate = gate_ref[r, :]
        o_ref[r, :] = (o * (gate * jax.nn.sigmoid(gate))).astype(o_ref.dtype)
        return new_state_t

    lax.fori_loop(0, n_chunks, chunk, jnp.zeros((GLA_DV, GLA_DK), F32))


def _gla(gq, gk, la, gv, gg, gain, tri2):
    B, H, T, DK = gq.shape
    hm = pl.BlockSpec((None, None, T, DK), lambda b, h: (b, h, 0, 0))
    col = pl.BlockSpec((None, T, GLA_DV), lambda b, h: (b, 0, h))
    return pl.pallas_call(
        _gla_kernel,
        grid=(B, H),
        in_specs=[hm, hm, hm, col, col, _const_spec(gain.shape), _const_spec(tri2.shape)],
        out_specs=col,
        out_shape=jax.ShapeDtypeStruct((B, T, GLA_V), BF16),
        compiler_params=pltpu.CompilerParams(
            dimension_semantics=("parallel", "parallel"), vmem_limit_bytes=VMEM_LIMIT_BYTES),
        name="gla",
    )(gq, gk, la, gv, gg, gain, tri2)


def _sb_kernel(qt_ref, k_ref, vt_ref, tri_ref, o_ref):
    S = SB_TILE
    i = pl.program_id(2)
    row = lax.broadcasted_iota(jnp.int32, (2 * SB_DH, S), 0)
    qt = qt_ref[...]
    q_heads = (jnp.where(row < SB_DH, qt, jnp.zeros_like(qt)),
               jnp.where(row >= SB_DH, qt, jnp.zeros_like(qt)))
    neg_tri2 = tri_ref[...]

    def block(j, carry, diagonal):
        kb = k_ref[pl.ds(pl.multiple_of(j * S, S), S), :]
        vt = vt_ref[j]
        out = []
        for hd in range(2):
            acc, rest = carry[hd]
            z = _dot(kb, q_heads[hd])
            sp = _softplus(z)
            if diagonal:
                keep = (lax.broadcasted_iota(jnp.int32, (S, S), 0)
                        < lax.broadcasted_iota(jnp.int32, (S, S), 1))
                sp = jnp.where(keep, sp, 0.0)
            hi, lo = _split_bf16(sp)
            ncum = _dot(neg_tri2, jnp.concatenate([hi, lo], axis=0))
            w = jnp.exp(z + ncum + rest)
            if diagonal:
                w = jnp.where(keep, w, 0.0)
            acc = acc + _dot(vt[hd * SB_DH:(hd + 1) * SB_DH, :], w.astype(BF16))
            out.append((acc, rest + ncum[0:1, :]))
        return tuple(out)

    zero = (jnp.zeros((SB_DH, S), F32), jnp.zeros((1, S), F32))
    carry = block(i, (zero, zero), True)
    carry = lax.fori_loop(0, i, lambda t, c: block(i - 1 - t, c, False), carry)
    o_ref[...] = jnp.concatenate([carry[0][0].T, carry[1][0].T], axis=1).astype(o_ref.dtype)


def _stick_breaking(sqt, sk, svt, neg_tri2):
    B, T, W = sk.shape
    S = SB_TILE
    nblk = T // S
    pairs = SB_HEADS // 2
    return pl.pallas_call(
        _sb_kernel,
        grid=(B, pairs, nblk),
        in_specs=[
            pl.BlockSpec((None, None, 2 * SB_DH, S), lambda b, p, i: (b, i, p, 0)),
            pl.BlockSpec((None, T, 2 * SB_DH), lambda b, p, i: (b, 0, p)),
            pl.BlockSpec((None, nblk, 2 * SB_DH, S), lambda b, p, i: (b, 0, p, 0)),
            _const_spec(neg_tri2.shape),
        ],
        out_specs=pl.BlockSpec((None, S, 2 * SB_DH), lambda b, p, i: (b, i, p)),
        out_shape=jax.ShapeDtypeStruct((B, T, W), BF16),
        compiler_params=pltpu.CompilerParams(
            dimension_semantics=("parallel", "parallel", "parallel"),
            vmem_limit_bytes=VMEM_LIMIT_BYTES),
        name="stick_breaking",
    )(sqt, sk, svt, neg_tri2)


def _out_ffn_kernel(x1_ref, og_ref, os_ref, wo_ref, g_ref, wg_ref, wu_ref, wd_ref, y_ref):
    x2 = (x1_ref[...] + _dot(og_ref[...], wo_ref[:GLA_V, :]) + _dot(os_ref[...], wo_ref[GLA_V:, :]))
    y_ref[...] = _swiglu_half_step(x2, g_ref, wg_ref, wu_ref, wd_ref)


def _out_ffn(x1, og, osb, wo, g, wg, wu, wd, tm):
    B, T, D = x1.shape
    tok = lambda w: pl.BlockSpec((None, tm, w), lambda b, i: (b, i, 0))
    consts = (wo, g, wg, wu, wd)
    return pl.pallas_call(
        _out_ffn_kernel,
        grid=(B, T // tm),
        in_specs=[tok(D), tok(GLA_V), tok(SB_W)] + [_const_spec(c.shape) for c in consts],
        out_specs=tok(D),
        out_shape=jax.ShapeDtypeStruct((B, T, D), F32),
        compiler_params=pltpu.CompilerParams(
            dimension_semantics=("parallel", "parallel"), vmem_limit_bytes=VMEM_LIMIT_BYTES),
        name="out_ffn2",
    )(x1, og, osb, *consts)


def _tri_pair(n, lower, value):
    r = np.arange(n)[:, None]
    c = np.arange(n)[None, :]
    m = np.where((c <= r) if lower else (c >= r), value, 0.0).astype(np.float32)
    return jnp.asarray(np.concatenate([m, m], axis=1), dtype=BF16)


def kernel(x, ffn1_norm, ffn1_w_gate, ffn1_w_up, ffn1_w_down, mix_norm, w_in, w_gk_up, b_gk,
           gla_out_norm, sb_q_norm, sb_k_norm, w_out, ffn2_norm, ffn2_w_gate, ffn2_w_up, ffn2_w_down):
    B, T, D = x.shape
    tm = min(TOKEN_TILE, T)
    assert T % tm == 0 and tm % SB_TILE == 0 and T % GLA_CHUNK == 0
    depth = w_in.shape[0]
    ones_bd = jnp.asarray(np.kron(np.eye(SB_HEADS), np.ones((SB_DH, SB_DH))), dtype=BF16)
    gla_tri2 = _tri_pair(GLA_CHUNK, True, 1.0)
    sb_neg_tri2 = _tri_pair(SB_TILE, False, -1.0)
    row = lambda v: v.reshape(1, -1).astype(F32)

    for l in range(depth):
        g_q, g_k, g_v, g_gate, g_lr, s_q, s_k, s_v = jnp.split(
            w_in[l], np.cumsum((GLA_QK, GLA_QK, GLA_V, GLA_V, GLA_RANK, SB_W, SB_W))[:].tolist(), axis=1)
        lr_pad = jnp.pad(g_lr, ((0, 0), (0, RANK_PAD - GLA_RANK)))
        wp = jnp.concatenate([g_q, g_k, g_v, g_gate, s_q, s_k, s_v, lr_pad], axis=1).astype(BF16)
        wup = jnp.pad(w_gk_up[l], ((0, RANK_PAD - GLA_RANK), (0, 0))).astype(BF16)

        x1, gq, gk, la, gv, gg, sqt, sk, svt = _ffn_proj(
            x, row(ffn1_norm[l]), ffn1_w_gate[l].astype(BF16), ffn1_w_up[l].astype(BF16),
            ffn1_w_down[l].astype(BF16), row(mix_norm[l]), wp, wup, row(b_gk[l]),
            row(jnp.tile(sb_q_norm[l], SB_HEADS)), row(jnp.tile(sb_k_norm[l], SB_HEADS)), ones_bd, tm)
        o_gla = _gla(gq, gk, la, gv, gg, row(gla_out_norm[l]), gla_tri2)
        o_sb = _stick_breaking(sqt, sk, svt, sb_neg_tri2)
        x = _out_ffn(x1, o_gla, o_sb, w_out[l].astype(BF16), row(ffn2_norm[l]),
                     ffn2_w_gate[l].astype(BF16), ffn2_w_up[l].astype(BF16),
                     ffn2_w_down[l].astype(BF16), tm)
    return x
```

```python
import functools

import jax
import jax.numpy as jnp
import numpy as np
from jax import lax
from jax.experimental import pallas as pl
from jax.experimental.pallas import tpu as pltpu

EPS = 1e-6
GLA_HEADS = 4
GLA_DK = 64
GLA_DV = 128
GLA_RANK = 16
GLA_GATE_NORM = 16.0
GLA_CHUNK = 64
SB_HEADS = 8
SB_DH = 64
GLA_QK = GLA_HEADS * GLA_DK
GLA_V = GLA_HEADS * GLA_DV
SB_W = SB_HEADS * SB_DH

LANES = 128
RANK_PAD = LANES
VMEM_LIMIT_BYTES = 56 * 1024 * 1024

TOKEN_TILE = 512
SB_TILE = 256
SB_KEYS = 128
SB_STAGES = 3
SB_MASKED = -1e30
FFN_CHUNKS = 2

LOG2E = 1.4426950408889634
F32 = jnp.float32
BF16 = jnp.bfloat16

_C_GQ = 0
_C_GK = _C_GQ + GLA_QK
_C_GV = _C_GK + GLA_QK
_C_GG = _C_GV + GLA_V
_C_SQ = _C_GG + GLA_V
_C_SK = _C_SQ + SB_W
_C_SV = _C_SK + SB_W
_C_LR = _C_SV + SB_W
_C_END = _C_LR + RANK_PAD


def _dot(a, b):
    return jnp.dot(a, b, preferred_element_type=F32)


def _dot_nt(a, b):
    return lax.dot_general(a, b, (((1,), (1,)), ((), ())), preferred_element_type=F32)


def _dot_tn(a, b):
    return lax.dot_general(a, b, (((0,), (0,)), ((), ())), preferred_element_type=F32)


def _rms(x, g):
    return x * lax.rsqrt(jnp.mean(x * x, axis=-1, keepdims=True) + EPS) * g


def _softplus(z):
    return jnp.maximum(z, 0.0) + jnp.log(1.0 + jnp.exp2(jnp.abs(z) * (-LOG2E)))


def _split_bf16(x):
    hi = x.astype(BF16)
    lo = (x - hi.astype(F32)).astype(BF16)
    return hi, lo


def _swiglu_half_step(x, g_ref, wg_ref, wu_ref, wd_ref):
    h = _rms(x, g_ref[...]).astype(BF16)
    d_ff = wg_ref.shape[1]
    step = d_ff // FFN_CHUNKS
    acc = jnp.zeros(x.shape, F32)
    for c in range(FFN_CHUNKS):
        lo, hi = c * step, (c + 1) * step
        gate = _dot(h, wg_ref[:, lo:hi])
        up = _dot(h, wu_ref[:, lo:hi])
        act = (gate * jax.nn.sigmoid(gate) * up).astype(BF16)
        acc = acc + _dot(act, wd_ref[lo:hi, :])
    return x + 0.5 * acc


def _ffn_proj_kernel(x_ref, g1_ref, wg_ref, wu_ref, wd_ref, g2_ref, wp_ref, wup_ref, bgk_ref,
                     sqg_ref, skg_ref, ones_ref,
                     x1_ref, gq_ref, gk_ref, la_ref, gv_ref, gg_ref, sqt_ref, sk_ref, svt_ref):
    x1 = _swiglu_half_step(x_ref[...], g1_ref, wg_ref, wu_ref, wd_ref)
    x1_ref[...] = x1
    h = _rms(x1, g2_ref[...]).astype(BF16)

    def proj(lo, hi):
        return _dot(h, wp_ref[:, lo:hi])

    gq = proj(_C_GQ, _C_GK) * (GLA_DK ** -0.5)
    gk = proj(_C_GK, _C_GV)
    lr = proj(_C_LR, _C_END).astype(BF16)
    pre = _dot(lr, wup_ref[...]) + bgk_ref[...]
    la = -_softplus(-pre) * (1.0 / GLA_GATE_NORM)
    for hd in range(GLA_HEADS):
        sl = slice(hd * GLA_DK, (hd + 1) * GLA_DK)
        gq_ref[hd] = gq[:, sl]
        gk_ref[hd] = gk[:, sl]
        la_ref[hd] = la[:, sl]
    gv_ref[...] = proj(_C_GV, _C_GG).astype(BF16)
    gg_ref[...] = proj(_C_GG, _C_SQ)

    def head_norm(p, g_ref):
        ms = _dot((p * p).astype(BF16), ones_ref[...]) * (1.0 / SB_DH)
        return p * lax.rsqrt(ms + EPS) * g_ref[...]

    n_sub = sqt_ref.shape[0]
    sq = head_norm(proj(_C_SQ, _C_SK), sqg_ref) * (SB_DH ** -0.5)
    sqt = sq.T.astype(BF16)
    sk_ref[...] = head_norm(proj(_C_SK, _C_SV), skg_ref).astype(BF16)
    svt = proj(_C_SV, _C_LR).T.astype(BF16)
    for c in range(n_sub):
        sqt_ref[c] = sqt[:, c * SB_TILE:(c + 1) * SB_TILE]
        svt_ref[c] = svt[:, c * SB_TILE:(c + 1) * SB_TILE]


def _const_spec(shape):
    nd = len(shape)
    return pl.BlockSpec(shape, lambda *_: (0,) * nd, pipeline_mode=pl.Buffered(1))


def _ffn_proj(x, g1, wg, wu, wd, g2, wp, wup, bgk, sqg, skg, ones_bd, tm):
    B, T, D = x.shape
    nt = T // tm
    n_sub = tm // SB_TILE
    nblk = T // SB_TILE
    tok = lambda w: pl.BlockSpec((None, tm, w), lambda b, i: (b, i, 0))
    headmajor = pl.BlockSpec((None, GLA_HEADS, tm, GLA_DK), lambda b, i: (b, 0, i, 0))
    transposed = pl.BlockSpec((None, n_sub, SB_W, SB_TILE), lambda b, i: (b, i, 0, 0))
    consts = (g1, wg, wu, wd, g2, wp, wup, bgk, sqg, skg, ones_bd)
    return pl.pallas_call(
        _ffn_proj_kernel,
        grid=(B, nt),
        in_specs=[tok(D)] + [_const_spec(c.shape) for c in consts],
        out_specs=[tok(D), headmajor, headmajor, headmajor, tok(GLA_V), tok(GLA_V),
                   transposed, tok(SB_W), transposed],
        out_shape=[
            jax.ShapeDtypeStruct((B, T, D), F32),
            jax.ShapeDtypeStruct((B, GLA_HEADS, T, GLA_DK), F32),
            jax.ShapeDtypeStruct((B, GLA_HEADS, T, GLA_DK), F32),
            jax.ShapeDtypeStruct((B, GLA_HEADS, T, GLA_DK), F32),
            jax.ShapeDtypeStruct((B, T, GLA_V), BF16),
            jax.ShapeDtypeStruct((B, T, GLA_V), F32),
            jax.ShapeDtypeStruct((B, nblk, SB_W, SB_TILE), BF16),
            jax.ShapeDtypeStruct((B, T, SB_W), BF16),
            jax.ShapeDtypeStruct((B, nblk, SB_W, SB_TILE), BF16),
        ],
        compiler_params=pltpu.CompilerParams(
            dimension_semantics=("parallel", "parallel"), vmem_limit_bytes=VMEM_LIMIT_BYTES),
        name="ffn1_proj",
    )(x, *consts)


def _gla_kernel(q_ref, k_ref, la_ref, v_ref, gate_ref, g_ref, tri_ref, o_ref):
    C = GLA_CHUNK
    n_chunks = q_ref.shape[0] // C
    row = lax.broadcasted_iota(jnp.int32, (C, C), 0)
    col = lax.broadcasted_iota(jnp.int32, (C, C), 1)
    causal = col <= row
    tri2 = tri_ref[...]
    gain = g_ref[...]

    def chunk(n, state_t):
        r = pl.ds(pl.multiple_of(n * C, C), C)
        q = q_ref[r, :]
        k = k_ref[r, :]
        v = v_ref[r, :]
        la_hi, la_lo = _split_bf16(la_ref[r, :])
        b = _dot(tri2, jnp.concatenate([la_hi, la_lo], axis=0))
        b_last = b[C - 1:C, :]
        q_dec = (q * jnp.exp(b)).astype(BF16)
        k_inv = (k * jnp.exp(-b)).astype(BF16)
        k_end = (k * jnp.exp(b_last - b)).astype(BF16)
        scores = jnp.where(causal, _dot_nt(q_dec, k_inv), 0.0).astype(BF16)
        o = _dot(scores, v) + _dot_nt(q_dec, state_t.astype(BF16))
        new_state_t = state_t * jnp.exp(b_last) + _dot_tn(v, k_end)
        o = _rms(o, gain)
        gate = gate_ref[r, :]
        o_ref[r, :] = (o * (gate * jax.nn.sigmoid(gate))).astype(o_ref.dtype)
        return new_state_t

    lax.fori_loop(0, n_chunks, chunk, jnp.zeros((GLA_DV, GLA_DK), F32))


def _gla(gq, gk, la, gv, gg, gain, tri2):
    B, H, T, DK = gq.shape
    hm = pl.BlockSpec((None, None, T, DK), lambda b, h: (b, h, 0, 0))
    col = pl.BlockSpec((None, T, GLA_DV), lambda b, h: (b, 0, h))
    return pl.pallas_call(
        _gla_kernel,
        grid=(B, H),
        in_specs=[hm, hm, hm, col, col, _const_spec(gain.shape), _const_spec(tri2.shape)],
        out_specs=col,
        out_shape=jax.ShapeDtypeStruct((B, T, GLA_V), BF16),
        compiler_params=pltpu.CompilerParams(
            dimension_semantics=("parallel", "parallel"), vmem_limit_bytes=VMEM_LIMIT_BYTES),
        name="gla",
    )(gq, gk, la, gv, gg, gain, tri2)


def _sb_kernel(diag_ref, off_ref, qt_ref, k_ref, vt_ref, tri_ref, o_ref,
               qh_scr, z_scr, n_scr, acc_scr, rest_scr):
    S, K = SB_TILE, SB_KEYS
    n_sub = S // K
    nblk = qt_ref.shape[0]
    neg_tri2 = tri_ref[...]

    row = lax.broadcasted_iota(jnp.int32, (2 * SB_DH, S), 0)
    for blk in range(nblk):
        qt = qt_ref[blk]
        qh_scr[blk, 0] = jnp.where(row < SB_DH, qt, jnp.zeros_like(qt))
        qh_scr[blk, 1] = jnp.where(row >= SB_DH, qt, jnp.zeros_like(qt))
    z_scr[...] = jnp.zeros(z_scr.shape, F32)
    n_scr[...] = jnp.zeros(n_scr.shape, F32)

    def sweep(task_ref, n_tasks, diagonal):
        def step(u, phase):
            slot_a, slot_b, slot_c = phase, (phase + 2) % SB_STAGES, (phase + 1) % SB_STAGES

            i_a = task_ref[0, u]
            j_a = task_ref[1, u]
            for sub in range(n_sub):
                kb = k_ref[pl.ds(pl.multiple_of(j_a * S + sub * K, K), K), :]
                if diagonal:
                    keep = (lax.broadcasted_iota(jnp.int32, (K, S), 0) + sub * K
                            < lax.broadcasted_iota(jnp.int32, (K, S), 1))
                for hd in range(2):
                    z = _dot(kb, qh_scr[i_a, hd])
                    if diagonal:
                        z = jnp.where(keep, z, SB_MASKED)
                    z_scr[slot_a, sub, hd] = z

            i_c = task_ref[3, u]
            vt = vt_ref[task_ref[2, u]]
            for hd in range(2):
                if diagonal:
                    acc = jnp.zeros((SB_DH, S), F32)
                    rest = jnp.zeros((1, S), F32)
                else:
                    acc = acc_scr[i_c, hd]
                    rest = rest_scr[i_c, hd]
                for sub in reversed(range(n_sub)):
                    ncum = n_scr[slot_c, sub, hd]
                    w = jnp.exp(z_scr[slot_c, sub, hd] + ncum + rest)
                    v_t = vt[hd * SB_DH:(hd + 1) * SB_DH, sub * K:(sub + 1) * K]
                    acc = acc + _dot(v_t, w.astype(BF16))
                    rest = rest + ncum[0:1, :]
                acc_scr[i_c, hd] = acc
                rest_scr[i_c, hd] = rest

            for sub in range(n_sub):
                for hd in range(2):
                    hi, lo = _split_bf16(_softplus(z_scr[slot_b, sub, hd]))
                    n_scr[slot_b, sub, hd] = _dot(neg_tri2, jnp.concatenate([hi, lo], axis=0))

        def body(it, _):
            for phase in range(SB_STAGES):
                step(it * SB_STAGES + phase, phase)
            return 0

        lax.fori_loop(0, _sb_steps(n_tasks) // SB_STAGES, body, 0)

    sweep(diag_ref, nblk, True)
    if nblk > 1:
        sweep(off_ref, nblk * (nblk - 1) // 2, False)
    for blk in range(nblk):
        o_ref[blk * S:(blk + 1) * S, :] = jnp.concatenate(
            [acc_scr[blk, 0].T, acc_scr[blk, 1].T], axis=1).astype(o_ref.dtype)


def _sb_steps(n_tasks):
    return -(-(n_tasks + SB_STAGES - 1) // SB_STAGES) * SB_STAGES


def _sb_task_tables(nblk):
    def table(tasks):
        steps = _sb_steps(len(tasks))
        pad = [tasks[-1]] * (steps - len(tasks))
        stage_a = tasks + pad
        stage_c = [(nblk, tasks[0][1])] * 2 + tasks + [(nblk, tasks[-1][1])] * (steps - len(tasks) - 2)
        rows = [[t[0] for t in stage_a], [t[1] for t in stage_a],
                [t[1] for t in stage_c], [t[0] for t in stage_c]]
        return jnp.asarray(np.array(rows, dtype=np.int32))
    diag = [(i, i) for i in range(nblk)]
    off = [(i, j) for i in range(1, nblk) for j in range(i - 1, -1, -1)]
    return table(diag), table(off if off else diag)


def _stick_breaking(sqt, sk, svt, neg_tri2):
    B, T, W = sk.shape
    S, K = SB_TILE, SB_KEYS
    nblk = T // S
    pairs = SB_HEADS // 2
    diag_tab, off_tab = _sb_task_tables(nblk)
    grid_spec = pltpu.PrefetchScalarGridSpec(
        num_scalar_prefetch=2,
        grid=(B, pairs),
        in_specs=[
            pl.BlockSpec((None, nblk, 2 * SB_DH, S), lambda b, p, *_: (b, 0, p, 0)),
            pl.BlockSpec((None, T, 2 * SB_DH), lambda b, p, *_: (b, 0, p)),
            pl.BlockSpec((None, nblk, 2 * SB_DH, S), lambda b, p, *_: (b, 0, p, 0)),
            pl.BlockSpec(neg_tri2.shape, lambda b, p, *_: (0, 0)),
        ],
        out_specs=pl.BlockSpec((None, T, 2 * SB_DH), lambda b, p, *_: (b, 0, p)),
        scratch_shapes=[
            pltpu.VMEM((nblk, 2, 2 * SB_DH, S), BF16),
            pltpu.VMEM((SB_STAGES, S // K, 2, K, S), F32),
            pltpu.VMEM((SB_STAGES, S // K, 2, K, S), F32),
            pltpu.VMEM((nblk + 1, 2, SB_DH, S), F32),
            pltpu.VMEM((nblk + 1, 2, 1, S), F32),
        ],
    )
    return pl.pallas_call(
        _sb_kernel,
        grid_spec=grid_spec,
        out_shape=jax.ShapeDtypeStruct((B, T, W), BF16),
        compiler_params=pltpu.CompilerParams(
            dimension_semantics=("parallel", "parallel"), vmem_limit_bytes=VMEM_LIMIT_BYTES),
        name="stick_breaking",
    )(diag_tab, off_tab, sqt, sk, svt, neg_tri2)


def _out_ffn_kernel(x1_ref, og_ref, os_ref, wo_ref, g_ref, wg_ref, wu_ref, wd_ref, y_ref):
    x2 = (x1_ref[...] + _dot(og_ref[...], wo_ref[:GLA_V, :]) + _dot(os_ref[...], wo_ref[GLA_V:, :]))
    y_ref[...] = _swiglu_half_step(x2, g_ref, wg_ref, wu_ref, wd_ref)


def _out_ffn(x1, og, osb, wo, g, wg, wu, wd, tm):
    B, T, D = x1.shape
    tok = lambda w: pl.BlockSpec((None, tm, w), lambda b, i: (b, i, 0))
    consts = (wo, g, wg, wu, wd)
    return pl.pallas_call(
        _out_ffn_kernel,
        grid=(B, T // tm),
        in_specs=[tok(D), tok(GLA_V), tok(SB_W)] + [_const_spec(c.shape) for c in consts],
        out_specs=tok(D),
        out_shape=jax.ShapeDtypeStruct((B, T, D), F32),
        compiler_params=pltpu.CompilerParams(
            dimension_semantics=("parallel", "parallel"), vmem_limit_bytes=VMEM_LIMIT_BYTES),
        name="out_ffn2",
    )(x1, og, osb, *consts)


def _tri_pair(n, lower, value):
    r = np.arange(n)[:, None]
    c = np.arange(n)[None, :]
    m = np.where((c <= r) if lower else (c >= r), value, 0.0).astype(np.float32)
    return jnp.asarray(np.concatenate([m, m], axis=1), dtype=BF16)


def kernel(x, ffn1_norm, ffn1_w_gate, ffn1_w_up, ffn1_w_down, mix_norm, w_in, w_gk_up, b_gk,
           gla_out_norm, sb_q_norm, sb_k_norm, w_out, ffn2_norm, ffn2_w_gate, ffn2_w_up, ffn2_w_down):
    B, T, D = x.shape
    tm = min(TOKEN_TILE, T)
    assert T % tm == 0 and tm % SB_TILE == 0 and T % GLA_CHUNK == 0
    depth = w_in.shape[0]
    ones_bd = jnp.asarray(np.kron(np.eye(SB_HEADS), np.ones((SB_DH, SB_DH))), dtype=BF16)
    gla_tri2 = _tri_pair(GLA_CHUNK, True, 1.0)
    sb_neg_tri2 = _tri_pair(SB_KEYS, False, -1.0)
    row = lambda v: v.reshape(1, -1).astype(F32)

    for l in range(depth):
        g_q, g_k, g_v, g_gate, g_lr, s_q, s_k, s_v = jnp.split(
            w_in[l], np.cumsum((GLA_QK, GLA_QK, GLA_V, GLA_V, GLA_RANK, SB_W, SB_W))[:].tolist(), axis=1)
        lr_pad = jnp.pad(g_lr, ((0, 0), (0, RANK_PAD - GLA_RANK)))
        wp = jnp.concatenate([g_q, g_k, g_v, g_gate, s_q, s_k, s_v, lr_pad], axis=1).astype(BF16)
        wup = jnp.pad(w_gk_up[l], ((0, RANK_PAD - GLA_RANK), (0, 0))).astype(BF16)

        x1, gq, gk, la, gv, gg, sqt, sk, svt = _ffn_proj(
            x, row(ffn1_norm[l]), ffn1_w_gate[l].astype(BF16), ffn1_w_up[l].astype(BF16),
            ffn1_w_down[l].astype(BF16), row(mix_norm[l]), wp, wup, row(b_gk[l]),
            row(jnp.tile(sb_q_norm[l], SB_HEADS)), row(jnp.tile(sb_k_norm[l], SB_HEADS)), ones_bd, tm)
        o_gla = _gla(gq, gk, la, gv, gg, row(gla_out_norm[l]), gla_tri2)
        o_sb = _stick_breaking(sqt, sk, svt, sb_neg_tri2)
        x = _out_ffn(x1, o_gla, o_sb, w_out[l].astype(BF16), row(ffn2_norm[l]),
                     ffn2_w_gate[l].astype(BF16), ffn2_w_up[l].astype(BF16),
                     ffn2_w_down[l].astype(BF16), tm)
    return x
```

```python
import jax
import jax.numpy as jnp
import numpy as np
from jax import lax
from jax.experimental import pallas as pl
from jax.experimental.pallas import tpu as pltpu

EPS = 1e-6
GLA_HEADS = 4
GLA_DK = 64
GLA_DV = 128
GLA_RANK = 16
GLA_GATE_NORM = 16.0
GLA_CHUNK = 64
GLA_GROUP = 4
GLA_TIME_TILE = 1024
SB_HEADS = 8
SB_DH = 64
GLA_QK = GLA_HEADS * GLA_DK
GLA_V = GLA_HEADS * GLA_DV
SB_W = SB_HEADS * SB_DH

LANES = 128
MXU_TILE = 256
RANK_PAD = LANES
VMEM_LIMIT_BYTES = 56 * 1024 * 1024

TOKEN_TILE = 512
SB_TILE = 256
SB_KEYS = 128
SB_STAGES = 3
SB_BODY_STEPS = 6
SB_ROWS = 16
SB_MASKED = -1e30
FFN_CHUNKS = 2

LOG2E = 1.4426950408889634
F32 = jnp.float32
BF16 = jnp.bfloat16

_C_GQ = 0
_C_GK = _C_GQ + GLA_QK
_C_GV = _C_GK + GLA_QK
_C_GG = _C_GV + GLA_V
_C_SQ = _C_GG + GLA_V
_C_SK = _C_SQ + SB_W
_C_SV = _C_SK + SB_W
_C_LR = _C_SV + SB_W
_C_END = _C_LR + RANK_PAD


def _dot(a, b):
    return jnp.dot(a, b, preferred_element_type=F32)


def _dot_nt(a, b):
    return lax.dot_general(a, b, (((1,), (1,)), ((), ())), preferred_element_type=F32)


def _dot_tn(a, b):
    return lax.dot_general(a, b, (((0,), (0,)), ((), ())), preferred_element_type=F32)


def _rms(x, g):
    return x * lax.rsqrt(jnp.mean(x * x, axis=-1, keepdims=True) + EPS) * g


def _softplus_log2(z2):
    neg_abs = lax.bitcast_convert_type(
        lax.bitcast_convert_type(z2, jnp.uint32) | jnp.uint32(0x80000000), F32)
    return jnp.maximum(z2, 0.0) + jnp.log(1.0 + jnp.exp2(neg_abs)) * LOG2E


def _softplus(z):
    return jnp.maximum(z, 0.0) + jnp.log(1.0 + jnp.exp2(jnp.abs(z) * (-LOG2E)))


def _split_bf16(x):
    hi = x.astype(BF16)
    lo = (x - hi.astype(F32)).astype(BF16)
    return hi, lo


def _ffn_chunks(d_ff):
    tiles = -(-d_ff // MXU_TILE)
    edges = [min(d_ff, -(-tiles * c // FFN_CHUNKS) * MXU_TILE) for c in range(FFN_CHUNKS + 1)]
    return list(zip(edges[:-1], edges[1:]))


def _swiglu_half_step(x, g_ref, wg_ref, wu_ref, wd_ref):
    h = _rms(x, g_ref[...]).astype(BF16)
    acc = jnp.zeros(x.shape, F32)
    for lo, hi in _ffn_chunks(wg_ref.shape[1]):
        gate = _dot(h, wg_ref[:, lo:hi])
        up = _dot(h, wu_ref[:, lo:hi])
        act = (gate * jax.nn.sigmoid(gate) * up).astype(BF16)
        acc = acc + _dot(act, wd_ref[lo:hi, :])
    return x + 0.5 * acc


def _ffn_proj_kernel(x_ref, g1_ref, wg_ref, wu_ref, wd_ref, g2_ref, wp_ref, wup_ref, bgk_ref,
                     sqg_ref, skg_ref, ones_ref,
                     x1_ref, gq_ref, gk_ref, la_ref, gv_ref, gg_ref, sqt_ref, sk_ref, svt_ref):
    x1 = _swiglu_half_step(x_ref[...], g1_ref, wg_ref, wu_ref, wd_ref)
    x1_ref[...] = x1
    h = _rms(x1, g2_ref[...]).astype(BF16)

    def proj(lo, hi):
        return _dot(h, wp_ref[:, lo:hi])

    gq = proj(_C_GQ, _C_GK) * (GLA_DK ** -0.5)
    gk = proj(_C_GK, _C_GV)
    lr = proj(_C_LR, _C_END).astype(BF16)
    pre = _dot(lr, wup_ref[...]) + bgk_ref[...]
    la = -_softplus(-pre) * (1.0 / GLA_GATE_NORM)
    for hd in range(GLA_HEADS):
        sl = slice(hd * GLA_DK, (hd + 1) * GLA_DK)
        gq_ref[hd] = gq[:, sl]
        gk_ref[hd] = gk[:, sl]
        la_ref[hd] = la[:, sl]
    gv_ref[...] = proj(_C_GV, _C_GG).astype(BF16)
    gg_ref[...] = proj(_C_GG, _C_SQ)

    def head_norm(p, g_ref):
        sq = (p * p).astype(BF16)
        ms = jnp.concatenate(
            [_dot(sq[:, c:c + MXU_TILE], ones_ref[c:c + MXU_TILE, c:c + MXU_TILE])
             for c in range(0, SB_W, MXU_TILE)], axis=1) * (1.0 / SB_DH)
        return p * lax.rsqrt(ms + EPS) * g_ref[...]

    n_sub = sqt_ref.shape[0]
    sq = head_norm(proj(_C_SQ, _C_SK), sqg_ref) * (SB_DH ** -0.5 * LOG2E)
    sqt = sq.T.astype(BF16)
    sk_ref[...] = head_norm(proj(_C_SK, _C_SV), skg_ref).astype(BF16)
    svt = proj(_C_SV, _C_LR).T.astype(BF16)
    for c in range(n_sub):
        sqt_ref[c] = sqt[:, c * SB_TILE:(c + 1) * SB_TILE]
        svt_ref[c] = svt[:, c * SB_TILE:(c + 1) * SB_TILE]


def _const_spec(shape):
    nd = len(shape)
    return pl.BlockSpec(shape, lambda *_: (0,) * nd, pipeline_mode=pl.Buffered(1))


def _ffn_proj(x, g1, wg, wu, wd, g2, wp, wup, bgk, sqg, skg, ones_bd, tm):
    B, T, D = x.shape
    nt = T // tm
    n_sub = tm // SB_TILE
    nblk = T // SB_TILE
    tok = lambda w: pl.BlockSpec((None, tm, w), lambda b, i: (b, i, 0))
    headmajor = pl.BlockSpec((None, GLA_HEADS, tm, GLA_DK), lambda b, i: (b, 0, i, 0))
    transposed = pl.BlockSpec((None, n_sub, SB_W, SB_TILE), lambda b, i: (b, i, 0, 0))
    consts = (g1, wg, wu, wd, g2, wp, wup, bgk, sqg, skg, ones_bd)
    return pl.pallas_call(
        _ffn_proj_kernel,
        grid=(B, nt),
        in_specs=[tok(D)] + [_const_spec(c.shape) for c in consts],
        out_specs=[tok(D), headmajor, headmajor, headmajor, tok(GLA_V), tok(GLA_V),
                   transposed, tok(SB_W), transposed],
        out_shape=[
            jax.ShapeDtypeStruct((B, T, D), F32),
            jax.ShapeDtypeStruct((B, GLA_HEADS, T, GLA_DK), F32),
            jax.ShapeDtypeStruct((B, GLA_HEADS, T, GLA_DK), F32),
            jax.ShapeDtypeStruct((B, GLA_HEADS, T, GLA_DK), F32),
            jax.ShapeDtypeStruct((B, T, GLA_V), BF16),
            jax.ShapeDtypeStruct((B, T, GLA_V), F32),
            jax.ShapeDtypeStruct((B, nblk, SB_W, SB_TILE), BF16),
            jax.ShapeDtypeStruct((B, T, SB_W), BF16),
            jax.ShapeDtypeStruct((B, nblk, SB_W, SB_TILE), BF16),
        ],
        compiler_params=pltpu.CompilerParams(
            dimension_semantics=("parallel", "parallel"), vmem_limit_bytes=VMEM_LIMIT_BYTES),
        name="ffn1_proj",
    )(x, *consts)


def _gla_kernel(q_ref, k_ref, la_ref, v_ref, gate_ref, g_ref, tri_ref, o_ref, state_scr):
    C, H, G = GLA_CHUNK, GLA_HEADS, GLA_GROUP
    n_steps = q_ref.shape[1] // (C * G)
    row = lax.broadcasted_iota(jnp.int32, (C, C), 0)
    col = lax.broadcasted_iota(jnp.int32, (C, C), 1)
    causal = col <= row
    tri2 = tri_ref[...]
    gain = g_ref[...]

    @pl.when(pl.program_id(1) == 0)
    def _():
        state_scr[...] = jnp.zeros(state_scr.shape, F32)

    def step(n, _):
        pairs = [(h, c) for h in range(H) for c in range(G)]
        rows = {c: pl.ds(pl.multiple_of((n * G + c) * C, C), C) for c in range(G)}
        b = {}
        for h, c in pairs:
            hi, lo = _split_bf16(la_ref[h, rows[c], :])
            b[h, c] = _dot(tri2, jnp.concatenate([hi, lo], axis=0))
        q_dec, k_inv, k_end, decay, v = {}, {}, {}, {}, {}
        for h, c in pairs:
            b_last = b[h, c][C - 1:C, :]
            k = k_ref[h, rows[c], :]
            q_dec[h, c] = (q_ref[h, rows[c], :] * jnp.exp(b[h, c])).astype(BF16)
            k_inv[h, c] = (k * jnp.exp(-b[h, c])).astype(BF16)
            k_end[h, c] = (k * jnp.exp(b_last - b[h, c])).astype(BF16)
            decay[h, c] = jnp.exp(b_last)
            v[h, c] = v_ref[rows[c], h * GLA_DV:(h + 1) * GLA_DV]
        scores = {p: _dot_nt(q_dec[p], k_inv[p]) for p in pairs}
        kv_t = {p: _dot_tn(v[p], k_end[p]) for p in pairs}
        state_in = {}
        for h in range(H):
            st = state_scr[h]
            for c in range(G):
                state_in[h, c] = st.astype(BF16)
                st = st * decay[h, c] + kv_t[h, c]
            state_scr[h] = st
        out = {}
        for p in pairs:
            sc = jnp.where(causal, scores[p], 0.0).astype(BF16)
            out[p] = _dot(sc, v[p]) + _dot_nt(q_dec[p], state_in[p])
        for h, c in pairs:
            gate = gate_ref[rows[c], h * GLA_DV:(h + 1) * GLA_DV]
            o = _rms(out[h, c], gain) * (gate * jax.nn.sigmoid(gate))
            o_ref[rows[c], h * GLA_DV:(h + 1) * GLA_DV] = o.astype(o_ref.dtype)
        return 0

    lax.fori_loop(0, n_steps, step, 0)


def _gla(gq, gk, la, gv, gg, gain, tri2):
    B, H, T, DK = gq.shape
    tt = min(GLA_TIME_TILE, T)
    hm = pl.BlockSpec((None, H, tt, DK), lambda b, t: (b, 0, t, 0))
    tok = pl.BlockSpec((None, tt, GLA_V), lambda b, t: (b, t, 0))
    return pl.pallas_call(
        _gla_kernel,
        grid=(B, T // tt),
        in_specs=[hm, hm, hm, tok, tok, _const_spec(gain.shape), _const_spec(tri2.shape)],
        out_specs=tok,
        out_shape=jax.ShapeDtypeStruct((B, T, GLA_V), BF16),
        scratch_shapes=[pltpu.VMEM((H, GLA_DV, DK), F32)],
        compiler_params=pltpu.CompilerParams(
            dimension_semantics=("parallel", "arbitrary"), vmem_limit_bytes=VMEM_LIMIT_BYTES),
        name="gla",
    )(gq, gk, la, gv, gg, gain, tri2)


def _sb_kernel(diag_ref, off_ref, qt_ref, k_ref, vt_ref, tri_ref, o_ref,
               qh_scr, z_scr, n_scr, acc_scr, rest_scr):
    S, K = SB_TILE, SB_KEYS
    n_sub = S // K
    nblk = qt_ref.shape[0]
    neg_tri2 = tri_ref[...]

    row = lax.broadcasted_iota(jnp.int32, (2 * SB_DH, S), 0)
    for blk in range(nblk):
        qt = qt_ref[blk]
        qh_scr[blk, 0] = jnp.where(row < SB_DH, qt, jnp.zeros_like(qt))
        qh_scr[blk, 1] = jnp.where(row >= SB_DH, qt, jnp.zeros_like(qt))
    z_scr[...] = jnp.zeros(z_scr.shape, F32)
    n_scr[...] = jnp.zeros(n_scr.shape, F32)

    def sweep(task_ref, n_tasks, diagonal):
        def step(u, phase):
            slot_a, slot_b, slot_c = phase % SB_STAGES, (phase + 2) % SB_STAGES, (phase + 1) % SB_STAGES

            i_a = task_ref[0, u]
            j_a = task_ref[1, u]
            for sub in range(n_sub):
                kb = k_ref[pl.ds(pl.multiple_of(j_a * S + sub * K, K), K), :]
                if diagonal:
                    keep = (lax.broadcasted_iota(jnp.int32, (K, S), 0) + sub * K
                            < lax.broadcasted_iota(jnp.int32, (K, S), 1))
                for hd in range(2):
                    z = _dot(kb, qh_scr[i_a, hd])
                    if diagonal:
                        z = jnp.where(keep, z, SB_MASKED)
                    z_scr[slot_a, sub, hd] = z

            for sub in range(n_sub):
                for hd in range(2):
                    parts = []
                    for r0 in range(0, K, SB_ROWS):
                        hi, lo = _split_bf16(_softplus_log2(z_scr[slot_b, sub, hd, r0:r0 + SB_ROWS, :]))
                        parts += [hi, lo]
                    n_scr[slot_b, sub, hd] = _dot(neg_tri2, jnp.concatenate(parts, axis=0))

            i_c = task_ref[3, u]
            vt = vt_ref[task_ref[2, u]]
            for hd in range(2):
                if diagonal:
                    acc = jnp.zeros((SB_DH, S), F32)
                    rest = jnp.zeros((1, S), F32)
                else:
                    acc = acc_scr[i_c, hd]
                    rest = rest_scr[i_c, hd]
                for sub in reversed(range(n_sub)):
                    ncum = n_scr[slot_c, sub, hd]
                    w = jnp.exp2(z_scr[slot_c, sub, hd] + ncum)
                    v_t = vt[hd * SB_DH:(hd + 1) * SB_DH, sub * K:(sub + 1) * K]
                    acc = acc + _dot(v_t, w.astype(BF16)) * jnp.exp2(rest)
                    rest = rest + ncum[0:1, :]
                acc_scr[i_c, hd] = acc
                rest_scr[i_c, hd] = rest

        def body(it, _):
            for phase in range(SB_BODY_STEPS):
                step(it * SB_BODY_STEPS + phase, phase)
            return 0

        lax.fori_loop(0, _sb_steps(n_tasks) // SB_BODY_STEPS, body, 0)

    sweep(diag_ref, nblk, True)
    if nblk > 1:
        sweep(off_ref, nblk * (nblk - 1) // 2, False)
    for blk in range(nblk):
        o_ref[blk * S:(blk + 1) * S, :] = jnp.concatenate(
            [acc_scr[blk, 0].T, acc_scr[blk, 1].T], axis=1).astype(o_ref.dtype)


def _sb_steps(n_tasks):
    return -(-(n_tasks + SB_STAGES - 1) // SB_BODY_STEPS) * SB_BODY_STEPS


def _sb_task_tables(nblk):
    def table(tasks):
        steps = _sb_steps(len(tasks))
        pad = [tasks[-1]] * (steps - len(tasks))
        stage_a = tasks + pad
        stage_c = [(nblk, tasks[0][1])] * 2 + tasks + [(nblk, tasks[-1][1])] * (steps - len(tasks) - 2)
        rows = [[t[0] for t in stage_a], [t[1] for t in stage_a],
                [t[1] for t in stage_c], [t[0] for t in stage_c]]
        return jnp.asarray(np.array(rows, dtype=np.int32))
    diag = [(i, i) for i in range(nblk)]
    off = [(i, j) for i in range(1, nblk) for j in range(i - 1, -1, -1)]
    return table(diag), table(off if off else diag)


def _stick_breaking(sqt, sk, svt, neg_tri2):
    B, T, W = sk.shape
    S, K = SB_TILE, SB_KEYS
    nblk = T // S
    pairs = SB_HEADS // 2
    diag_tab, off_tab = _sb_task_tables(nblk)
    grid_spec = pltpu.PrefetchScalarGridSpec(
        num_scalar_prefetch=2,
        grid=(B, pairs),
        in_specs=[
            pl.BlockSpec((None, nblk, 2 * SB_DH, S), lambda b, p, *_: (b, 0, p, 0)),
            pl.BlockSpec((None, T, 2 * SB_DH), lambda b, p, *_: (b, 0, p)),
            pl.BlockSpec((None, nblk, 2 * SB_DH, S), lambda b, p, *_: (b, 0, p, 0)),
            pl.BlockSpec(neg_tri2.shape, lambda b, p, *_: (0, 0)),
        ],
        out_specs=pl.BlockSpec((None, T, 2 * SB_DH), lambda b, p, *_: (b, 0, p)),
        scratch_shapes=[
            pltpu.VMEM((nblk, 2, 2 * SB_DH, S), BF16),
            pltpu.VMEM((SB_STAGES, S // K, 2, K, S), F32),
            pltpu.VMEM((SB_STAGES, S // K, 2, K, S), F32),
            pltpu.VMEM((nblk + 1, 2, SB_DH, S), F32),
            pltpu.VMEM((nblk + 1, 2, 1, S), F32),
        ],
    )
    return pl.pallas_call(
        _sb_kernel,
        grid_spec=grid_spec,
        out_shape=jax.ShapeDtypeStruct((B, T, W), BF16),
        compiler_params=pltpu.CompilerParams(
            dimension_semantics=("parallel", "parallel"), vmem_limit_bytes=VMEM_LIMIT_BYTES),
        name="stick_breaking",
    )(diag_tab, off_tab, sqt, sk, svt, neg_tri2)


def _out_ffn_kernel(x1_ref, og_ref, os_ref, wo_ref, g_ref, wg_ref, wu_ref, wd_ref, y_ref):
    x2 = (x1_ref[...] + _dot(og_ref[...], wo_ref[:GLA_V, :]) + _dot(os_ref[...], wo_ref[GLA_V:, :]))
    y_ref[...] = _swiglu_half_step(x2, g_ref, wg_ref, wu_ref, wd_ref)


def _out_ffn(x1, og, osb, wo, g, wg, wu, wd, tm):
    B, T, D = x1.shape
    tok = lambda w: pl.BlockSpec((None, tm, w), lambda b, i: (b, i, 0))
    consts = (wo, g, wg, wu, wd)
    return pl.pallas_call(
        _out_ffn_kernel,
        grid=(B, T // tm),
        in_specs=[tok(D), tok(GLA_V), tok(SB_W)] + [_const_spec(c.shape) for c in consts],
        out_specs=tok(D),
        out_shape=jax.ShapeDtypeStruct((B, T, D), F32),
        compiler_params=pltpu.CompilerParams(
            dimension_semantics=("parallel", "parallel"), vmem_limit_bytes=VMEM_LIMIT_BYTES),
        name="out_ffn2",
    )(x1, og, osb, *consts)


def _tri_pair(n, lower, value):
    r = np.arange(n)[:, None]
    c = np.arange(n)[None, :]
    m = np.where((c <= r) if lower else (c >= r), value, 0.0).astype(np.float32)
    return jnp.asarray(np.concatenate([m, m], axis=1), dtype=BF16)


def _tri_interleaved(n, group, value):
    r = np.arange(n)[:, None]
    c = np.arange(n)[None, :]
    m = np.where(c >= r, value, 0.0).astype(np.float32).reshape(n, n // group, 1, group)
    return jnp.asarray(np.broadcast_to(m, (n, n // group, 2, group)).reshape(n, 2 * n), dtype=BF16)


def kernel(x, ffn1_norm, ffn1_w_gate, ffn1_w_up, ffn1_w_down, mix_norm, w_in, w_gk_up, b_gk,
           gla_out_norm, sb_q_norm, sb_k_norm, w_out, ffn2_norm, ffn2_w_gate, ffn2_w_up, ffn2_w_down):
    B, T, D = x.shape
    tm = min(TOKEN_TILE, T)
    assert T % tm == 0 and tm % SB_TILE == 0 and T % (GLA_CHUNK * GLA_GROUP) == 0
    depth = w_in.shape[0]
    ones_bd = jnp.asarray(np.kron(np.eye(SB_HEADS), np.ones((SB_DH, SB_DH))), dtype=BF16)
    gla_tri2 = _tri_pair(GLA_CHUNK, True, 1.0)
    sb_neg_tri2 = _tri_interleaved(SB_KEYS, SB_ROWS, -1.0)
    row = lambda v: v.reshape(1, -1).astype(F32)

    for l in range(depth):
        g_q, g_k, g_v, g_gate, g_lr, s_q, s_k, s_v = jnp.split(
            w_in[l], np.cumsum((GLA_QK, GLA_QK, GLA_V, GLA_V, GLA_RANK, SB_W, SB_W))[:].tolist(), axis=1)
        lr_pad = jnp.pad(g_lr, ((0, 0), (0, RANK_PAD - GLA_RANK)))
        wp = jnp.concatenate([g_q, g_k, g_v, g_gate, s_q, s_k, s_v, lr_pad], axis=1).astype(BF16)
        wup = jnp.pad(w_gk_up[l], ((0, RANK_PAD - GLA_RANK), (0, 0))).astype(BF16)

        x1, gq, gk, la, gv, gg, sqt, sk, svt = _ffn_proj(
            x, row(ffn1_norm[l]), ffn1_w_gate[l].astype(BF16), ffn1_w_up[l].astype(BF16),
            ffn1_w_down[l].astype(BF16), row(mix_norm[l]), wp, wup, row(b_gk[l]),
            row(jnp.tile(sb_q_norm[l], SB_HEADS)), row(jnp.tile(sb_k_norm[l], SB_HEADS)), ones_bd, tm)
        o_gla = _gla(gq, gk, la, gv, gg, row(gla_out_norm[l]), gla_tri2)
        o_sb = _stick_breaking(sqt, sk, svt, sb_neg_tri2)
        x = _out_ffn(x1, o_gla, o_sb, w_out[l].astype(BF16), row(ffn2_norm[l]),
                     ffn2_w_gate[l].astype(BF16), ffn2_w_up[l].astype(BF16),
                     ffn2_w_down[l].astype(BF16), tm)
    return x
```

```python
import jax
import jax.numpy as jnp
import numpy as np
from jax import lax
from jax.experimental import pallas as pl
from jax.experimental.pallas import tpu as pltpu

EPS = 1e-6
GLA_HEADS = 4
GLA_DK = 64
GLA_DV = 128
GLA_RANK = 16
GLA_GATE_NORM = 16.0
GLA_CHUNK = 64
GLA_GROUP = 4
GLA_TIME_TILE = 1024
SB_HEADS = 8
SB_DH = 64
GLA_QK = GLA_HEADS * GLA_DK
GLA_V = GLA_HEADS * GLA_DV
SB_W = SB_HEADS * SB_DH

LANES = 128
MXU_TILE = 256
RANK_PAD = LANES
VMEM_LIMIT_BYTES = 56 * 1024 * 1024

TOKEN_TILE = 512
SB_TILE = 256
SB_KEYS = 128
SB_STAGES = 3
SB_BODY_STEPS = 9
SB_ROWS = 16
SB_MASKED = -1e30
FFN_CHUNKS = 2

LOG2E = 1.4426950408889634
SOFTPLUS_CLAMP = 64.0
F32 = jnp.float32
BF16 = jnp.bfloat16

_C_GQ = 0
_C_GK = _C_GQ + GLA_QK
_C_GV = _C_GK + GLA_QK
_C_GG = _C_GV + GLA_V
_C_SQ = _C_GG + GLA_V
_C_SK = _C_SQ + SB_W
_C_SV = _C_SK + SB_W
_C_LR = _C_SV + SB_W
_C_END = _C_LR + RANK_PAD


def _dot(a, b):
    return jnp.dot(a, b, preferred_element_type=F32)


def _dot_nt(a, b):
    return lax.dot_general(a, b, (((1,), (1,)), ((), ())), preferred_element_type=F32)


def _dot_tn(a, b):
    return lax.dot_general(a, b, (((0,), (0,)), ((), ())), preferred_element_type=F32)


def _rms(x, g):
    return x * lax.rsqrt(jnp.mean(x * x, axis=-1, keepdims=True) + EPS) * g


def _softplus_log2(z2):
    return jnp.maximum(z2, jnp.log(1.0 + jnp.exp2(jnp.minimum(z2, SOFTPLUS_CLAMP))) * LOG2E)


def _softplus(z):
    return jnp.maximum(z, 0.0) + jnp.log(1.0 + jnp.exp2(jnp.abs(z) * (-LOG2E)))


def _split_bf16(x):
    hi = x.astype(BF16)
    lo = (x - hi.astype(F32)).astype(BF16)
    return hi, lo


def _ffn_chunks(d_ff):
    tiles = -(-d_ff // MXU_TILE)
    edges = [min(d_ff, -(-tiles * c // FFN_CHUNKS) * MXU_TILE) for c in range(FFN_CHUNKS + 1)]
    return list(zip(edges[:-1], edges[1:]))


def _swiglu_half_step(x, g_ref, wg_ref, wu_ref, wd_ref):
    h = _rms(x, g_ref[...]).astype(BF16)
    acc = jnp.zeros(x.shape, F32)
    for lo, hi in _ffn_chunks(wg_ref.shape[1]):
        gate = _dot(h, wg_ref[:, lo:hi])
        up = _dot(h, wu_ref[:, lo:hi])
        act = (gate * jax.nn.sigmoid(gate) * up).astype(BF16)
        acc = acc + _dot(act, wd_ref[lo:hi, :])
    return x + 0.5 * acc


def _ffn_proj_kernel(x_ref, g1_ref, wg_ref, wu_ref, wd_ref, g2_ref, wp_ref, wup_ref, bgk_ref,
                     sqg_ref, skg_ref, ones_ref,
                     x1_ref, gq_ref, gk_ref, la_ref, gv_ref, gg_ref, sqt_ref, sk_ref, svt_ref):
    x1 = _swiglu_half_step(x_ref[...], g1_ref, wg_ref, wu_ref, wd_ref)
    x1_ref[...] = x1
    h = _rms(x1, g2_ref[...]).astype(BF16)

    def proj(lo, hi):
        return _dot(h, wp_ref[:, lo:hi])

    gq = proj(_C_GQ, _C_GK) * (GLA_DK ** -0.5)
    gk = proj(_C_GK, _C_GV)
    lr = proj(_C_LR, _C_END).astype(BF16)
    pre = _dot(lr, wup_ref[...]) + bgk_ref[...]
    la = -_softplus(-pre) * (1.0 / GLA_GATE_NORM)
    for hd in range(GLA_HEADS):
        sl = slice(hd * GLA_DK, (hd + 1) * GLA_DK)
        gq_ref[hd] = gq[:, sl]
        gk_ref[hd] = gk[:, sl]
        la_ref[hd] = la[:, sl]
    gv_ref[...] = proj(_C_GV, _C_GG).astype(BF16)
    gg_ref[...] = proj(_C_GG, _C_SQ)

    def head_norm(p, g_ref):
        sq = (p * p).astype(BF16)
        ms = jnp.concatenate(
            [_dot(sq[:, c:c + MXU_TILE], ones_ref[c:c + MXU_TILE, c:c + MXU_TILE])
             for c in range(0, SB_W, MXU_TILE)], axis=1) * (1.0 / SB_DH)
        return p * lax.rsqrt(ms + EPS) * g_ref[...]

    n_sub = sqt_ref.shape[0]
    sq = head_norm(proj(_C_SQ, _C_SK), sqg_ref) * (SB_DH ** -0.5 * LOG2E)
    sqt = sq.T.astype(BF16)
    sk_ref[...] = head_norm(proj(_C_SK, _C_SV), skg_ref).astype(BF16)
    svt = proj(_C_SV, _C_LR).T.astype(BF16)
    for c in range(n_sub):
        sqt_ref[c] = sqt[:, c * SB_TILE:(c + 1) * SB_TILE]
        svt_ref[c] = svt[:, c * SB_TILE:(c + 1) * SB_TILE]


def _const_spec(shape):
    nd = len(shape)
    return pl.BlockSpec(shape, lambda *_: (0,) * nd, pipeline_mode=pl.Buffered(1))


def _ffn_proj(x, g1, wg, wu, wd, g2, wp, wup, bgk, sqg, skg, ones_bd, tm):
    B, T, D = x.shape
    nt = T // tm
    n_sub = tm // SB_TILE
    nblk = T // SB_TILE
    tok = lambda w: pl.BlockSpec((None, tm, w), lambda b, i: (b, i, 0))
    headmajor = pl.BlockSpec((None, GLA_HEADS, tm, GLA_DK), lambda b, i: (b, 0, i, 0))
    transposed = pl.BlockSpec((None, n_sub, SB_W, SB_TILE), lambda b, i: (b, i, 0, 0))
    consts = (g1, wg, wu, wd, g2, wp, wup, bgk, sqg, skg, ones_bd)
    return pl.pallas_call(
        _ffn_proj_kernel,
        grid=(B, nt),
        in_specs=[tok(D)] + [_const_spec(c.shape) for c in consts],
        out_specs=[tok(D), headmajor, headmajor, headmajor, tok(GLA_V), tok(GLA_V),
                   transposed, tok(SB_W), transposed],
        out_shape=[
            jax.ShapeDtypeStruct((B, T, D), F32),
            jax.ShapeDtypeStruct((B, GLA_HEADS, T, GLA_DK), F32),
            jax.ShapeDtypeStruct((B, GLA_HEADS, T, GLA_DK), F32),
            jax.ShapeDtypeStruct((B, GLA_HEADS, T, GLA_DK), F32),
            jax.ShapeDtypeStruct((B, T, GLA_V), BF16),
            jax.ShapeDtypeStruct((B, T, GLA_V), F32),
            jax.ShapeDtypeStruct((B, nblk, SB_W, SB_TILE), BF16),
            jax.ShapeDtypeStruct((B, T, SB_W), BF16),
            jax.ShapeDtypeStruct((B, nblk, SB_W, SB_TILE), BF16),
        ],
        compiler_params=pltpu.CompilerParams(
            dimension_semantics=("parallel", "parallel"), vmem_limit_bytes=VMEM_LIMIT_BYTES),
        name="ffn1_proj",
    )(x, *consts)


def _gla_kernel(q_ref, k_ref, la_ref, v_ref, gate_ref, g_ref, tri_ref, o_ref, state_scr):
    C, H, G = GLA_CHUNK, GLA_HEADS, GLA_GROUP
    n_steps = q_ref.shape[1] // (C * G)
    row = lax.broadcasted_iota(jnp.int32, (C, C), 0)
    col = lax.broadcasted_iota(jnp.int32, (C, C), 1)
    causal = col <= row
    tri2 = tri_ref[...]
    gain = g_ref[...]

    @pl.when(pl.program_id(1) == 0)
    def _():
        state_scr[...] = jnp.zeros(state_scr.shape, F32)

    def step(n, _):
        pairs = [(h, c) for h in range(H) for c in range(G)]
        rows = {c: pl.ds(pl.multiple_of((n * G + c) * C, C), C) for c in range(G)}
        b = {}
        for h, c in pairs:
            hi, lo = _split_bf16(la_ref[h, rows[c], :])
            b[h, c] = _dot(tri2, jnp.concatenate([hi, lo], axis=0))
        q_dec, k_inv, k_end, decay, v = {}, {}, {}, {}, {}
        for h, c in pairs:
            b_last = b[h, c][C - 1:C, :]
            k = k_ref[h, rows[c], :]
            q_dec[h, c] = (q_ref[h, rows[c], :] * jnp.exp(b[h, c])).astype(BF16)
            k_inv[h, c] = (k * jnp.exp(-b[h, c])).astype(BF16)
            k_end[h, c] = (k * jnp.exp(b_last - b[h, c])).astype(BF16)
            decay[h, c] = jnp.exp(b_last)
            v[h, c] = v_ref[rows[c], h * GLA_DV:(h + 1) * GLA_DV]
        scores = {p: _dot_nt(q_dec[p], k_inv[p]) for p in pairs}
        kv_t = {p: _dot_tn(v[p], k_end[p]) for p in pairs}
        state_in = {}
        for h in range(H):
            st = state_scr[h]
            for c in range(G):
                state_in[h, c] = st.astype(BF16)
                st = st * decay[h, c] + kv_t[h, c]
            state_scr[h] = st
        out = {}
        for p in pairs:
            sc = jnp.where(causal, scores[p], 0.0).astype(BF16)
            out[p] = _dot(sc, v[p]) + _dot_nt(q_dec[p], state_in[p])
        for h, c in pairs:
            gate = gate_ref[rows[c], h * GLA_DV:(h + 1) * GLA_DV]
            o = _rms(out[h, c], gain) * (gate * jax.nn.sigmoid(gate))
            o_ref[rows[c], h * GLA_DV:(h + 1) * GLA_DV] = o.astype(o_ref.dtype)
        return 0

    lax.fori_loop(0, n_steps, step, 0)


def _gla(gq, gk, la, gv, gg, gain, tri2):
    B, H, T, DK = gq.shape
    tt = min(GLA_TIME_TILE, T)
    hm = pl.BlockSpec((None, H, tt, DK), lambda b, t: (b, 0, t, 0))
    tok = pl.BlockSpec((None, tt, GLA_V), lambda b, t: (b, t, 0))
    return pl.pallas_call(
        _gla_kernel,
        grid=(B, T // tt),
        in_specs=[hm, hm, hm, tok, tok, _const_spec(gain.shape), _const_spec(tri2.shape)],
        out_specs=tok,
        out_shape=jax.ShapeDtypeStruct((B, T, GLA_V), BF16),
        scratch_shapes=[pltpu.VMEM((H, GLA_DV, DK), F32)],
        compiler_params=pltpu.CompilerParams(
            dimension_semantics=("parallel", "arbitrary"), vmem_limit_bytes=VMEM_LIMIT_BYTES),
        name="gla",
    )(gq, gk, la, gv, gg, gain, tri2)


def _sb_kernel(diag_ref, off_ref, qt_ref, k_ref, vt_ref, tri_ref, o_ref,
               qh_scr, z_scr, n_scr, acc_scr, rest_scr):
    S, K = SB_TILE, SB_KEYS
    n_sub = S // K
    nblk = qt_ref.shape[0]
    neg_tri2 = tri_ref[...]

    row = lax.broadcasted_iota(jnp.int32, (2 * SB_DH, S), 0)
    for blk in range(nblk):
        qt = qt_ref[blk]
        qh_scr[blk, 0] = jnp.where(row < SB_DH, qt, jnp.zeros_like(qt))
        qh_scr[blk, 1] = jnp.where(row >= SB_DH, qt, jnp.zeros_like(qt))
    z_scr[...] = jnp.zeros(z_scr.shape, F32)
    n_scr[...] = jnp.zeros(n_scr.shape, F32)

    def sweep(task_ref, n_tasks, diagonal):
        def step(u, phase):
            slot_a, slot_b, slot_c = phase % SB_STAGES, (phase + 2) % SB_STAGES, (phase + 1) % SB_STAGES

            i_a = task_ref[0, u]
            j_a = task_ref[1, u]
            for sub in range(n_sub):
                kb = k_ref[pl.ds(pl.multiple_of(j_a * S + sub * K, K), K), :]
                if diagonal:
                    keep = (lax.broadcasted_iota(jnp.int32, (K, S), 0) + sub * K
                            < lax.broadcasted_iota(jnp.int32, (K, S), 1))
                for hd in range(2):
                    z = _dot(kb, qh_scr[i_a, hd])
                    if diagonal:
                        z = jnp.where(keep, z, SB_MASKED)
                    z_scr[slot_a, sub, hd] = z

            for sub in range(n_sub):
                for hd in range(2):
                    parts = []
                    for r0 in range(0, K, SB_ROWS):
                        hi, lo = _split_bf16(_softplus_log2(z_scr[slot_b, sub, hd, r0:r0 + SB_ROWS, :]))
                        parts += [hi, lo]
                    n_scr[slot_b, sub, hd] = _dot(neg_tri2, jnp.concatenate(parts, axis=0))

            i_c = task_ref[3, u]
            vt = vt_ref[task_ref[2, u]]
            for hd in range(2):
                if diagonal:
                    acc = jnp.zeros((SB_DH, S), F32)
                    rest = jnp.zeros((1, S), F32)
                else:
                    acc = acc_scr[i_c, hd]
                    rest = rest_scr[i_c, hd]
                for sub in reversed(range(n_sub)):
                    ncum = n_scr[slot_c, sub, hd]
                    w = jnp.exp2(z_scr[slot_c, sub, hd] + ncum)
                    v_t = vt[hd * SB_DH:(hd + 1) * SB_DH, sub * K:(sub + 1) * K]
                    acc = acc + _dot(v_t, w.astype(BF16)) * jnp.exp2(rest)
                    rest = rest + ncum[0:1, :]
                acc_scr[i_c, hd] = acc
                rest_scr[i_c, hd] = rest

        def body(it, _):
            for phase in range(SB_BODY_STEPS):
                step(it * SB_BODY_STEPS + phase, phase)
            return 0

        lax.fori_loop(0, _sb_steps(n_tasks) // SB_BODY_STEPS, body, 0)

    sweep(diag_ref, nblk, True)
    if nblk > 1:
        sweep(off_ref, nblk * (nblk - 1) // 2, False)
    for blk in range(nblk):
        o_ref[blk * S:(blk + 1) * S, :] = jnp.concatenate(
            [acc_scr[blk, 0].T, acc_scr[blk, 1].T], axis=1).astype(o_ref.dtype)


def _sb_steps(n_tasks):
    return -(-(n_tasks + SB_STAGES - 1) // SB_BODY_STEPS) * SB_BODY_STEPS


def _sb_task_tables(nblk):
    def table(tasks):
        steps = _sb_steps(len(tasks))
        pad = [tasks[-1]] * (steps - len(tasks))
        stage_a = tasks + pad
        stage_c = [(nblk, tasks[0][1])] * 2 + tasks + [(nblk, tasks[-1][1])] * (steps - len(tasks) - 2)
        rows = [[t[0] for t in stage_a], [t[1] for t in stage_a],
                [t[1] for t in stage_c], [t[0] for t in stage_c]]
        return jnp.asarray(np.array(rows, dtype=np.int32))
    diag = [(i, i) for i in range(nblk)]
    off = [(i, j) for i in range(1, nblk) for j in range(i - 1, -1, -1)]
    return table(diag), table(off if off else diag)


def _stick_breaking(sqt, sk, svt, neg_tri2):
    B, T, W = sk.shape
    S, K = SB_TILE, SB_KEYS
    nblk = T // S
    pairs = SB_HEADS // 2
    diag_tab, off_tab = _sb_task_tables(nblk)
    grid_spec = pltpu.PrefetchScalarGridSpec(
        num_scalar_prefetch=2,
        grid=(B, pairs),
        in_specs=[
            pl.BlockSpec((None, nblk, 2 * SB_DH, S), lambda b, p, *_: (b, 0, p, 0)),
            pl.BlockSpec((None, T, 2 * SB_DH), lambda b, p, *_: (b, 0, p)),
            pl.BlockSpec((None, nblk, 2 * SB_DH, S), lambda b, p, *_: (b, 0, p, 0)),
            pl.BlockSpec(neg_tri2.shape, lambda b, p, *_: (0, 0)),
        ],
        out_specs=pl.BlockSpec((None, T, 2 * SB_DH), lambda b, p, *_: (b, 0, p)),
        scratch_shapes=[
            pltpu.VMEM((nblk, 2, 2 * SB_DH, S), BF16),
            pltpu.VMEM((SB_STAGES, S // K, 2, K, S), F32),
            pltpu.VMEM((SB_STAGES, S // K, 2, K, S), F32),
            pltpu.VMEM((nblk + 1, 2, SB_DH, S), F32),
            pltpu.VMEM((nblk + 1, 2, 1, S), F32),
        ],
    )
    return pl.pallas_call(
        _sb_kernel,
        grid_spec=grid_spec,
        out_shape=jax.ShapeDtypeStruct((B, T, W), BF16),
        compiler_params=pltpu.CompilerParams(
            dimension_semantics=("parallel", "parallel"), vmem_limit_bytes=VMEM_LIMIT_BYTES),
        name="stick_breaking",
    )(diag_tab, off_tab, sqt, sk, svt, neg_tri2)


def _out_ffn_kernel(x1_ref, og_ref, os_ref, wo_ref, g_ref, wg_ref, wu_ref, wd_ref, y_ref):
    x2 = (x1_ref[...] + _dot(og_ref[...], wo_ref[:GLA_V, :]) + _dot(os_ref[...], wo_ref[GLA_V:, :]))
    y_ref[...] = _swiglu_half_step(x2, g_ref, wg_ref, wu_ref, wd_ref)


def _out_ffn(x1, og, osb, wo, g, wg, wu, wd, tm):
    B, T, D = x1.shape
    tok = lambda w: pl.BlockSpec((None, tm, w), lambda b, i: (b, i, 0))
    consts = (wo, g, wg, wu, wd)
    return pl.pallas_call(
        _out_ffn_kernel,
        grid=(B, T // tm),
        in_specs=[tok(D), tok(GLA_V), tok(SB_W)] + [_const_spec(c.shape) for c in consts],
        out_specs=tok(D),
        out_shape=jax.ShapeDtypeStruct((B, T, D), F32),
        compiler_params=pltpu.CompilerParams(
            dimension_semantics=("parallel", "parallel"), vmem_limit_bytes=VMEM_LIMIT_BYTES),
        name="out_ffn2",
    )(x1, og, osb, *consts)


def _tri_pair(n, lower, value):
    r = np.arange(n)[:, None]
    c = np.arange(n)[None, :]
    m = np.where((c <= r) if lower else (c >= r), value, 0.0).astype(np.float32)
    return jnp.asarray(np.concatenate([m, m], axis=1), dtype=BF16)


def _tri_interleaved(n, group, value):
    r = np.arange(n)[:, None]
    c = np.arange(n)[None, :]
    m = np.where(c >= r, value, 0.0).astype(np.float32).reshape(n, n // group, 1, group)
    return jnp.asarray(np.broadcast_to(m, (n, n // group, 2, group)).reshape(n, 2 * n), dtype=BF16)


def kernel(x, ffn1_norm, ffn1_w_gate, ffn1_w_up, ffn1_w_down, mix_norm, w_in, w_gk_up, b_gk,
           gla_out_norm, sb_q_norm, sb_k_norm, w_out, ffn2_norm, ffn2_w_gate, ffn2_w_up, ffn2_w_down):
    B, T, D = x.shape
    tm = min(TOKEN_TILE, T)
    assert T % tm == 0 and tm % SB_TILE == 0 and T % (GLA_CHUNK * GLA_GROUP) == 0
    depth = w_in.shape[0]
    ones_bd = jnp.asarray(np.kron(np.eye(SB_HEADS), np.ones((SB_DH, SB_DH))), dtype=BF16)
    gla_tri2 = _tri_pair(GLA_CHUNK, True, 1.0)
    sb_neg_tri2 = _tri_interleaved(SB_KEYS, SB_ROWS, -1.0)
    row = lambda v: v.reshape(1, -1).astype(F32)

    for l in range(depth):
        g_q, g_k, g_v, g_gate, g_lr, s_q, s_k, s_v = jnp.split(
            w_in[l], np.cumsum((GLA_QK, GLA_QK, GLA_V, GLA_V, GLA_RANK, SB_W, SB_W))[:].tolist(), axis=1)
        lr_pad = jnp.pad(g_lr, ((0, 0), (0, RANK_PAD - GLA_RANK)))
        wp = jnp.concatenate([g_q, g_k, g_v, g_gate, s_q, s_k, s_v, lr_pad], axis=1).astype(BF16)
        wup = jnp.pad(w_gk_up[l], ((0, RANK_PAD - GLA_RANK), (0, 0))).astype(BF16)

        x1, gq, gk, la, gv, gg, sqt, sk, svt = _ffn_proj(
            x, row(ffn1_norm[l]), ffn1_w_gate[l].astype(BF16), ffn1_w_up[l].astype(BF16),
            ffn1_w_down[l].astype(BF16), row(mix_norm[l]), wp, wup, row(b_gk[l]),
            row(jnp.tile(sb_q_norm[l], SB_HEADS)), row(jnp.tile(sb_k_norm[l], SB_HEADS)), ones_bd, tm)
        o_gla = _gla(gq, gk, la, gv, gg, row(gla_out_norm[l]), gla_tri2)
        o_sb = _stick_breaking(sqt, sk, svt, sb_neg_tri2)
        x = _out_ffn(x1, o_gla, o_sb, w_out[l].astype(BF16), row(ffn2_norm[l]),
                     ffn2_w_gate[l].astype(BF16), ffn2_w_up[l].astype(BF16),
                     ffn2_w_down[l].astype(BF16), tm)
    return x
```

```python
import jax
import jax.numpy as jnp
import numpy as np
from jax import lax
from jax.experimental import pallas as pl
from jax.experimental.pallas import tpu as pltpu

EPS = 1e-6
GLA_HEADS = 4
GLA_DK = 64
GLA_DV = 128
GLA_RANK = 16
GLA_GATE_NORM = 16.0
GLA_CHUNK = 64
GLA_GROUP = 4
GLA_TIME_TILE = 1024
SB_HEADS = 8
SB_DH = 64
GLA_QK = GLA_HEADS * GLA_DK
GLA_V = GLA_HEADS * GLA_DV
SB_W = SB_HEADS * SB_DH

LANES = 128
MXU_TILE = 256
RANK_PAD = LANES
VMEM_LIMIT_BYTES = 56 * 1024 * 1024

TOKEN_TILE = 512
SB_TILE = 256
SB_KEYS = 128
SB_STAGES = 3
SB_BODY_STEPS = 18
SB_ROWS = 16
SB_MASKED = -1e30
FFN_CHUNKS = 2

LOG2E = 1.4426950408889634
SOFTPLUS_CLAMP = 64.0
F32 = jnp.float32
BF16 = jnp.bfloat16

_C_GQ = 0
_C_GK = _C_GQ + GLA_QK
_C_GV = _C_GK + GLA_QK
_C_GG = _C_GV + GLA_V
_C_SQ = _C_GG + GLA_V
_C_SK = _C_SQ + SB_W
_C_SV = _C_SK + SB_W
_C_LR = _C_SV + SB_W
_C_END = _C_LR + RANK_PAD


def _dot(a, b):
    return jnp.dot(a, b, preferred_element_type=F32)


def _dot_nt(a, b):
    return lax.dot_general(a, b, (((1,), (1,)), ((), ())), preferred_element_type=F32)


def _dot_tn(a, b):
    return lax.dot_general(a, b, (((0,), (0,)), ((), ())), preferred_element_type=F32)


def _rms(x, g):
    return x * lax.rsqrt(jnp.mean(x * x, axis=-1, keepdims=True) + EPS) * g


def _softplus_log2(z2):
    return jnp.maximum(z2, jnp.log(1.0 + jnp.exp2(jnp.minimum(z2, SOFTPLUS_CLAMP))) * LOG2E)


def _softplus(z):
    return jnp.maximum(z, 0.0) + jnp.log(1.0 + jnp.exp2(jnp.abs(z) * (-LOG2E)))


def _split_bf16(x):
    hi = x.astype(BF16)
    lo = (x - hi.astype(F32)).astype(BF16)
    return hi, lo


def _ffn_chunks(d_ff):
    tiles = -(-d_ff // MXU_TILE)
    edges = [min(d_ff, -(-tiles * c // FFN_CHUNKS) * MXU_TILE) for c in range(FFN_CHUNKS + 1)]
    return list(zip(edges[:-1], edges[1:]))


def _swiglu_half_step(x, g_ref, wg_ref, wu_ref, wd_ref):
    h = _rms(x, g_ref[...]).astype(BF16)
    acc = jnp.zeros(x.shape, F32)
    for lo, hi in _ffn_chunks(wg_ref.shape[1]):
        gate = _dot(h, wg_ref[:, lo:hi])
        up = _dot(h, wu_ref[:, lo:hi])
        act = (gate * jax.nn.sigmoid(gate) * up).astype(BF16)
        acc = acc + _dot(act, wd_ref[lo:hi, :])
    return x + 0.5 * acc


def _ffn_proj_kernel(x_ref, g1_ref, wg_ref, wu_ref, wd_ref, g2_ref, wp_ref, wup_ref, bgk_ref,
                     sqg_ref, skg_ref, ones_ref,
                     x1_ref, gq_ref, gk_ref, la_ref, gv_ref, gg_ref, sqt_ref, sk_ref, svt_ref):
    x1 = _swiglu_half_step(x_ref[...], g1_ref, wg_ref, wu_ref, wd_ref)
    x1_ref[...] = x1
    h = _rms(x1, g2_ref[...]).astype(BF16)

    def proj(lo, hi):
        return _dot(h, wp_ref[:, lo:hi])

    lr = proj(_C_LR, _C_END).astype(BF16)
    sq = proj(_C_SQ, _C_SK)
    sk = proj(_C_SK, _C_SV)
    svt = proj(_C_SV, _C_LR).T.astype(BF16)
    gq = proj(_C_GQ, _C_GK) * (GLA_DK ** -0.5)
    gk = proj(_C_GK, _C_GV)

    def head_norm(p, g_ref):
        sq2 = (p * p).astype(BF16)
        ms = jnp.concatenate(
            [_dot(sq2[:, c:c + MXU_TILE], ones_ref[c:c + MXU_TILE, c:c + MXU_TILE])
             for c in range(0, SB_W, MXU_TILE)], axis=1) * (1.0 / SB_DH)
        return p * lax.rsqrt(ms + EPS) * g_ref[...]

    pre = _dot(lr, wup_ref[...]) + bgk_ref[...]
    sqt = (head_norm(sq, sqg_ref) * (SB_DH ** -0.5 * LOG2E)).T.astype(BF16)
    sk_ref[...] = head_norm(sk, skg_ref).astype(BF16)
    gv_ref[...] = proj(_C_GV, _C_GG).astype(BF16)
    gg_ref[...] = proj(_C_GG, _C_SQ)

    la = -_softplus(-pre) * (1.0 / GLA_GATE_NORM)
    for hd in range(GLA_HEADS):
        sl = slice(hd * GLA_DK, (hd + 1) * GLA_DK)
        gq_ref[hd] = gq[:, sl]
        gk_ref[hd] = gk[:, sl]
        la_ref[hd] = la[:, sl]
    for c in range(sqt_ref.shape[0]):
        sqt_ref[c] = sqt[:, c * SB_TILE:(c + 1) * SB_TILE]
        svt_ref[c] = svt[:, c * SB_TILE:(c + 1) * SB_TILE]


def _const_spec(shape):
    nd = len(shape)
    return pl.BlockSpec(shape, lambda *_: (0,) * nd, pipeline_mode=pl.Buffered(1))


def _ffn_proj(x, g1, wg, wu, wd, g2, wp, wup, bgk, sqg, skg, ones_bd, tm):
    B, T, D = x.shape
    nt = T // tm
    n_sub = tm // SB_TILE
    nblk = T // SB_TILE
    tok = lambda w: pl.BlockSpec((None, tm, w), lambda b, i: (b, i, 0))
    headmajor = pl.BlockSpec((None, GLA_HEADS, tm, GLA_DK), lambda b, i: (b, 0, i, 0))
    transposed = pl.BlockSpec((None, n_sub, SB_W, SB_TILE), lambda b, i: (b, i, 0, 0))
    consts = (g1, wg, wu, wd, g2, wp, wup, bgk, sqg, skg, ones_bd)
    return pl.pallas_call(
        _ffn_proj_kernel,
        grid=(B, nt),
        in_specs=[tok(D)] + [_const_spec(c.shape) for c in consts],
        out_specs=[tok(D), headmajor, headmajor, headmajor, tok(GLA_V), tok(GLA_V),
                   transposed, tok(SB_W), transposed],
        out_shape=[
            jax.ShapeDtypeStruct((B, T, D), F32),
            jax.ShapeDtypeStruct((B, GLA_HEADS, T, GLA_DK), F32),
            jax.ShapeDtypeStruct((B, GLA_HEADS, T, GLA_DK), F32),
            jax.ShapeDtypeStruct((B, GLA_HEADS, T, GLA_DK), F32),
            jax.ShapeDtypeStruct((B, T, GLA_V), BF16),
            jax.ShapeDtypeStruct((B, T, GLA_V), F32),
            jax.ShapeDtypeStruct((B, nblk, SB_W, SB_TILE), BF16),
            jax.ShapeDtypeStruct((B, T, SB_W), BF16),
            jax.ShapeDtypeStruct((B, nblk, SB_W, SB_TILE), BF16),
        ],
        compiler_params=pltpu.CompilerParams(
            dimension_semantics=("parallel", "parallel"), vmem_limit_bytes=VMEM_LIMIT_BYTES),
        name="ffn1_proj",
    )(x, *consts)


def _gla_kernel(q_ref, k_ref, la_ref, v_ref, gate_ref, g_ref, tri_ref, o_ref, state_scr):
    C, H, G = GLA_CHUNK, GLA_HEADS, GLA_GROUP
    n_steps = q_ref.shape[1] // (C * G)
    row = lax.broadcasted_iota(jnp.int32, (C, C), 0)
    col = lax.broadcasted_iota(jnp.int32, (C, C), 1)
    causal = col <= row
    tri2 = tri_ref[...]
    gain = g_ref[...]

    @pl.when(pl.program_id(1) == 0)
    def _():
        state_scr[...] = jnp.zeros(state_scr.shape, F32)

    def step(n, _):
        pairs = [(h, c) for h in range(H) for c in range(G)]
        rows = {c: pl.ds(pl.multiple_of((n * G + c) * C, C), C) for c in range(G)}
        b = {}
        for h, c in pairs:
            hi, lo = _split_bf16(la_ref[h, rows[c], :])
            b[h, c] = _dot(tri2, jnp.concatenate([hi, lo], axis=0))
        q_dec, k_inv, k_end, decay, v = {}, {}, {}, {}, {}
        for h, c in pairs:
            b_last = b[h, c][C - 1:C, :]
            k = k_ref[h, rows[c], :]
            q_dec[h, c] = (q_ref[h, rows[c], :] * jnp.exp(b[h, c])).astype(BF16)
            k_inv[h, c] = (k * jnp.exp(-b[h, c])).astype(BF16)
            k_end[h, c] = (k * jnp.exp(b_last - b[h, c])).astype(BF16)
            decay[h, c] = jnp.exp(b_last)
            v[h, c] = v_ref[rows[c], h * GLA_DV:(h + 1) * GLA_DV]
        scores = {p: _dot_nt(q_dec[p], k_inv[p]) for p in pairs}
        kv_t = {p: _dot_tn(v[p], k_end[p]) for p in pairs}
        state_in = {}
        for h in range(H):
            st = state_scr[h]
            for c in range(G):
                state_in[h, c] = st.astype(BF16)
                st = st * decay[h, c] + kv_t[h, c]
            state_scr[h] = st
        out = {}
        for p in pairs:
            sc = jnp.where(causal, scores[p], 0.0).astype(BF16)
            out[p] = _dot(sc, v[p]) + _dot_nt(q_dec[p], state_in[p])
        for h, c in pairs:
            gate = gate_ref[rows[c], h * GLA_DV:(h + 1) * GLA_DV]
            o = _rms(out[h, c], gain) * (gate * jax.nn.sigmoid(gate))
            o_ref[rows[c], h * GLA_DV:(h + 1) * GLA_DV] = o.astype(o_ref.dtype)
        return 0

    lax.fori_loop(0, n_steps, step, 0)


def _gla(gq, gk, la, gv, gg, gain, tri2):
    B, H, T, DK = gq.shape
    tt = min(GLA_TIME_TILE, T)
    hm = pl.BlockSpec((None, H, tt, DK), lambda b, t: (b, 0, t, 0))
    tok = pl.BlockSpec((None, tt, GLA_V), lambda b, t: (b, t, 0))
    return pl.pallas_call(
        _gla_kernel,
        grid=(B, T // tt),
        in_specs=[hm, hm, hm, tok, tok, _const_spec(gain.shape), _const_spec(tri2.shape)],
        out_specs=tok,
        out_shape=jax.ShapeDtypeStruct((B, T, GLA_V), BF16),
        scratch_shapes=[pltpu.VMEM((H, GLA_DV, DK), F32)],
        compiler_params=pltpu.CompilerParams(
            dimension_semantics=("parallel", "arbitrary"), vmem_limit_bytes=VMEM_LIMIT_BYTES),
        name="gla",
    )(gq, gk, la, gv, gg, gain, tri2)


def _sb_kernel(diag_ref, off_ref, qt_ref, k_ref, vt_ref, tri_ref, o_ref,
               qh_scr, z_scr, n_scr, acc_scr, rest_scr):
    S, K = SB_TILE, SB_KEYS
    n_sub = S // K
    nblk = qt_ref.shape[0]
    neg_tri2 = tri_ref[...]

    row = lax.broadcasted_iota(jnp.int32, (2 * SB_DH, S), 0)
    for blk in range(nblk):
        qt = qt_ref[blk]
        qh_scr[blk, 0] = jnp.where(row < SB_DH, qt, jnp.zeros_like(qt))
        qh_scr[blk, 1] = jnp.where(row >= SB_DH, qt, jnp.zeros_like(qt))
    z_scr[...] = jnp.zeros(z_scr.shape, F32)
    n_scr[...] = jnp.zeros(n_scr.shape, F32)

    def sweep(task_ref, n_tasks, diagonal):
        def step(u, phase):
            slot_a, slot_b, slot_c = phase % SB_STAGES, (phase + 2) % SB_STAGES, (phase + 1) % SB_STAGES

            i_a = task_ref[0, u]
            j_a = task_ref[1, u]
            for sub in range(n_sub):
                kb = k_ref[pl.ds(pl.multiple_of(j_a * S + sub * K, K), K), :]
                if diagonal:
                    keep = (lax.broadcasted_iota(jnp.int32, (K, S), 0) + sub * K
                            < lax.broadcasted_iota(jnp.int32, (K, S), 1))
                for hd in range(2):
                    z = _dot(kb, qh_scr[i_a, hd])
                    if diagonal:
                        z = jnp.where(keep, z, SB_MASKED)
                    z_scr[slot_a, sub, hd] = z

            for sub in range(n_sub):
                for hd in range(2):
                    parts = []
                    for r0 in range(0, K, SB_ROWS):
                        hi, lo = _split_bf16(_softplus_log2(z_scr[slot_b, sub, hd, r0:r0 + SB_ROWS, :]))
                        parts += [hi, lo]
                    n_scr[slot_b, sub, hd] = _dot(neg_tri2, jnp.concatenate(parts, axis=0))

            i_c = task_ref[3, u]
            vt = vt_ref[task_ref[2, u]]
            for hd in range(2):
                if diagonal:
                    acc = jnp.zeros((SB_DH, S), F32)
                    rest = jnp.zeros((1, S), F32)
                else:
                    acc = acc_scr[i_c, hd]
                    rest = rest_scr[i_c, hd]
                for sub in reversed(range(n_sub)):
                    ncum = n_scr[slot_c, sub, hd]
                    w = jnp.exp2(z_scr[slot_c, sub, hd] + ncum)
                    v_t = vt[hd * SB_DH:(hd + 1) * SB_DH, sub * K:(sub + 1) * K]
                    acc = acc + _dot(v_t, w.astype(BF16)) * jnp.exp2(rest)
                    rest = rest + ncum[0:1, :]
                acc_scr[i_c, hd] = acc
                rest_scr[i_c, hd] = rest

        def body(it, _):
            for phase in range(SB_BODY_STEPS):
                step(it * SB_BODY_STEPS + phase, phase)
            return 0

        lax.fori_loop(0, _sb_steps(n_tasks) // SB_BODY_STEPS, body, 0)

    sweep(diag_ref, nblk, True)
    if nblk > 1:
        sweep(off_ref, nblk * (nblk - 1) // 2, False)
    for blk in range(nblk):
        o_ref[blk * S:(blk + 1) * S, :] = jnp.concatenate(
            [acc_scr[blk, 0].T, acc_scr[blk, 1].T], axis=1).astype(o_ref.dtype)


def _sb_steps(n_tasks):
    return -(-(n_tasks + SB_STAGES - 1) // SB_BODY_STEPS) * SB_BODY_STEPS


def _sb_task_tables(nblk):
    def table(tasks):
        steps = _sb_steps(len(tasks))
        pad = [tasks[-1]] * (steps - len(tasks))
        stage_a = tasks + pad
        stage_c = [(nblk, tasks[0][1])] * 2 + tasks + [(nblk, tasks[-1][1])] * (steps - len(tasks) - 2)
        rows = [[t[0] for t in stage_a], [t[1] for t in stage_a],
                [t[1] for t in stage_c], [t[0] for t in stage_c]]
        return jnp.asarray(np.array(rows, dtype=np.int32))
    diag = [(i, i) for i in range(nblk)]
    off = [(i, j) for i in range(1, nblk) for j in range(i - 1, -1, -1)]
    return table(diag), table(off if off else diag)


def _stick_breaking(sqt, sk, svt, neg_tri2):
    B, T, W = sk.shape
    S, K = SB_TILE, SB_KEYS
    nblk = T // S
    pairs = SB_HEADS // 2
    diag_tab, off_tab = _sb_task_tables(nblk)
    grid_spec = pltpu.PrefetchScalarGridSpec(
        num_scalar_prefetch=2,
        grid=(B, pairs),
        in_specs=[
            pl.BlockSpec((None, nblk, 2 * SB_DH, S), lambda b, p, *_: (b, 0, p, 0)),
            pl.BlockSpec((None, T, 2 * SB_DH), lambda b, p, *_: (b, 0, p)),
            pl.BlockSpec((None, nblk, 2 * SB_DH, S), lambda b, p, *_: (b, 0, p, 0)),
            pl.BlockSpec(neg_tri2.shape, lambda b, p, *_: (0, 0)),
        ],
        out_specs=pl.BlockSpec((None, T, 2 * SB_DH), lambda b, p, *_: (b, 0, p)),
        scratch_shapes=[
            pltpu.VMEM((nblk, 2, 2 * SB_DH, S), BF16),
            pltpu.VMEM((SB_STAGES, S // K, 2, K, S), F32),
            pltpu.VMEM((SB_STAGES, S // K, 2, K, S), F32),
            pltpu.VMEM((nblk + 1, 2, SB_DH, S), F32),
            pltpu.VMEM((nblk + 1, 2, 1, S), F32),
        ],
    )
    return pl.pallas_call(
        _sb_kernel,
        grid_spec=grid_spec,
        out_shape=jax.ShapeDtypeStruct((B, T, W), BF16),
        compiler_params=pltpu.CompilerParams(
            dimension_semantics=("parallel", "parallel"), vmem_limit_bytes=VMEM_LIMIT_BYTES),
        name="stick_breaking",
    )(diag_tab, off_tab, sqt, sk, svt, neg_tri2)


def _out_ffn_kernel(x1_ref, og_ref, os_ref, wo_ref, g_ref, wg_ref, wu_ref, wd_ref, y_ref):
    x2 = (x1_ref[...] + _dot(og_ref[...], wo_ref[:GLA_V, :]) + _dot(os_ref[...], wo_ref[GLA_V:, :]))
    y_ref[...] = _swiglu_half_step(x2, g_ref, wg_ref, wu_ref, wd_ref)


def _out_ffn(x1, og, osb, wo, g, wg, wu, wd, tm):
    B, T, D = x1.shape
    tok = lambda w: pl.BlockSpec((None, tm, w), lambda b, i: (b, i, 0))
    consts = (wo, g, wg, wu, wd)
    return pl.pallas_call(
        _out_ffn_kernel,
        grid=(B, T // tm),
        in_specs=[tok(D), tok(GLA_V), tok(SB_W)] + [_const_spec(c.shape) for c in consts],
        out_specs=tok(D),
        out_shape=jax.ShapeDtypeStruct((B, T, D), F32),
        compiler_params=pltpu.CompilerParams(
            dimension_semantics=("parallel", "parallel"), vmem_limit_bytes=VMEM_LIMIT_BYTES),
        name="out_ffn2",
    )(x1, og, osb, *consts)


def _tri_pair(n, lower, value):
    r = np.arange(n)[:, None]
    c = np.arange(n)[None, :]
    m = np.where((c <= r) if lower else (c >= r), value, 0.0).astype(np.float32)
    return jnp.asarray(np.concatenate([m, m], axis=1), dtype=BF16)


def _tri_interleaved(n, group, value):
    r = np.arange(n)[:, None]
    c = np.arange(n)[None, :]
    m = np.where(c >= r, value, 0.0).astype(np.float32).reshape(n, n // group, 1, group)
    return jnp.asarray(np.broadcast_to(m, (n, n // group, 2, group)).reshape(n, 2 * n), dtype=BF16)


def kernel(x, ffn1_norm, ffn1_w_gate, ffn1_w_up, ffn1_w_down, mix_norm, w_in, w_gk_up, b_gk,
           gla_out_norm, sb_q_norm, sb_k_norm, w_out, ffn2_norm, ffn2_w_gate, ffn2_w_up, ffn2_w_down):
    B, T, D = x.shape
    tm = min(TOKEN_TILE, T)
    assert T % tm == 0 and tm % SB_TILE == 0 and T % (GLA_CHUNK * GLA_GROUP) == 0
    depth = w_in.shape[0]
    ones_bd = jnp.asarray(np.kron(np.eye(SB_HEADS), np.ones((SB_DH, SB_DH))), dtype=BF16)
    gla_tri2 = _tri_pair(GLA_CHUNK, True, 1.0)
    sb_neg_tri2 = _tri_interleaved(SB_KEYS, SB_ROWS, -1.0)
    row = lambda v: v.reshape(1, -1).astype(F32)

    for l in range(depth):
        g_q, g_k, g_v, g_gate, g_lr, s_q, s_k, s_v = jnp.split(
            w_in[l], np.cumsum((GLA_QK, GLA_QK, GLA_V, GLA_V, GLA_RANK, SB_W, SB_W))[:].tolist(), axis=1)
        lr_pad = jnp.pad(g_lr, ((0, 0), (0, RANK_PAD - GLA_RANK)))
        wp = jnp.concatenate([g_q, g_k, g_v, g_gate, s_q, s_k, s_v, lr_pad], axis=1).astype(BF16)
        wup = jnp.pad(w_gk_up[l], ((0, RANK_PAD - GLA_RANK), (0, 0))).astype(BF16)

        x1, gq, gk, la, gv, gg, sqt, sk, svt = _ffn_proj(
            x, row(ffn1_norm[l]), ffn1_w_gate[l].astype(BF16), ffn1_w_up[l].astype(BF16),
            ffn1_w_down[l].astype(BF16), row(mix_norm[l]), wp, wup, row(b_gk[l]),
            row(jnp.tile(sb_q_norm[l], SB_HEADS)), row(jnp.tile(sb_k_norm[l], SB_HEADS)), ones_bd, tm)
        o_gla = _gla(gq, gk, la, gv, gg, row(gla_out_norm[l]), gla_tri2)
        o_sb = _stick_breaking(sqt, sk, svt, sb_neg_tri2)
        x = _out_ffn(x1, o_gla, o_sb, w_out[l].astype(BF16), row(ffn2_norm[l]),
                     ffn2_w_gate[l].astype(BF16), ffn2_w_up[l].astype(BF16),
                     ffn2_w_down[l].astype(BF16), tm)
    return x
```

```python
import jax
import jax.numpy as jnp
import numpy as np
from jax import lax
from jax.experimental import pallas as pl
from jax.experimental.pallas import tpu as pltpu

EPS = 1e-6
GLA_HEADS = 4
GLA_DK = 64
GLA_DV = 128
GLA_RANK = 16
GLA_GATE_NORM = 16.0
GLA_CHUNK = 64
GLA_GROUP = 4
GLA_TIME_TILE = 1024
SB_HEADS = 8
SB_DH = 64
GLA_QK = GLA_HEADS * GLA_DK
GLA_V = GLA_HEADS * GLA_DV
SB_W = SB_HEADS * SB_DH

LANES = 128
MXU_TILE = 256
RANK_PAD = LANES
VMEM_LIMIT_BYTES = 56 * 1024 * 1024

TOKEN_TILE = 512
SB_TILE = 256
SB_KEYS = 128
SB_STAGES = 3
SB_BODY_STEPS = 30
SB_ROWS = 16
SB_MASKED = -1e30
FFN_CHUNKS = 2

LOG2E = 1.4426950408889634
SOFTPLUS_CLAMP = 64.0
F32 = jnp.float32
BF16 = jnp.bfloat16

_C_GQ = 0
_C_GK = _C_GQ + GLA_QK
_C_GV = _C_GK + GLA_QK
_C_GG = _C_GV + GLA_V
_C_SQ = _C_GG + GLA_V
_C_SK = _C_SQ + SB_W
_C_SV = _C_SK + SB_W
_C_LR = _C_SV + SB_W
_C_END = _C_LR + RANK_PAD


def _dot(a, b):
    return jnp.dot(a, b, preferred_element_type=F32)


def _dot_nt(a, b):
    return lax.dot_general(a, b, (((1,), (1,)), ((), ())), preferred_element_type=F32)


def _dot_tn(a, b):
    return lax.dot_general(a, b, (((0,), (0,)), ((), ())), preferred_element_type=F32)


def _rms(x, g):
    return x * lax.rsqrt(jnp.mean(x * x, axis=-1, keepdims=True) + EPS) * g


def _softplus_log2(z2):
    return jnp.maximum(z2, jnp.log(1.0 + jnp.exp2(jnp.minimum(z2, SOFTPLUS_CLAMP))) * LOG2E)


def _softplus(z):
    return jnp.maximum(z, 0.0) + jnp.log(1.0 + jnp.exp2(jnp.abs(z) * (-LOG2E)))


def _split_bf16(x):
    hi = x.astype(BF16)
    lo = (x - hi.astype(F32)).astype(BF16)
    return hi, lo


def _ffn_chunks(d_ff):
    tiles = -(-d_ff // MXU_TILE)
    edges = [min(d_ff, -(-tiles * c // FFN_CHUNKS) * MXU_TILE) for c in range(FFN_CHUNKS + 1)]
    return list(zip(edges[:-1], edges[1:]))


def _swiglu_half_step(x, g_ref, wg_ref, wu_ref, wd_ref):
    h = _rms(x, g_ref[...]).astype(BF16)
    acc = jnp.zeros(x.shape, F32)
    for lo, hi in _ffn_chunks(wg_ref.shape[1]):
        gate = _dot(h, wg_ref[:, lo:hi])
        up = _dot(h, wu_ref[:, lo:hi])
        act = (gate * jax.nn.sigmoid(gate) * up).astype(BF16)
        acc = acc + _dot(act, wd_ref[lo:hi, :])
    return x + 0.5 * acc


def _ffn_proj_kernel(x_ref, g1_ref, wg_ref, wu_ref, wd_ref, g2_ref, wp_ref, wup_ref, bgk_ref,
                     sqg_ref, skg_ref, ones_ref,
                     x1_ref, gq_ref, gk_ref, la_ref, gv_ref, gg_ref, sqt_ref, sk_ref, svt_ref):
    x1 = _swiglu_half_step(x_ref[...], g1_ref, wg_ref, wu_ref, wd_ref)
    x1_ref[...] = x1
    h = _rms(x1, g2_ref[...]).astype(BF16)

    def proj(lo, hi):
        return _dot(h, wp_ref[:, lo:hi])

    lr = proj(_C_LR, _C_END).astype(BF16)
    sq = proj(_C_SQ, _C_SK)
    sk = proj(_C_SK, _C_SV)
    svt = proj(_C_SV, _C_LR).T.astype(BF16)
    gq = proj(_C_GQ, _C_GK) * (GLA_DK ** -0.5)
    gk = proj(_C_GK, _C_GV)

    def head_norm(p, g_ref):
        sq2 = (p * p).astype(BF16)
        ms = jnp.concatenate(
            [_dot(sq2[:, c:c + MXU_TILE], ones_ref[c:c + MXU_TILE, c:c + MXU_TILE])
             for c in range(0, SB_W, MXU_TILE)], axis=1) * (1.0 / SB_DH)
        return p * lax.rsqrt(ms + EPS) * g_ref[...]

    pre = _dot(lr, wup_ref[...]) + bgk_ref[...]
    sqt = (head_norm(sq, sqg_ref) * (SB_DH ** -0.5 * LOG2E)).T.astype(BF16)
    sk_ref[...] = head_norm(sk, skg_ref).astype(BF16)
    gv_ref[...] = proj(_C_GV, _C_GG).astype(BF16)
    gg_ref[...] = proj(_C_GG, _C_SQ)

    la = -_softplus(-pre) * (1.0 / GLA_GATE_NORM)
    for hd in range(GLA_HEADS):
        sl = slice(hd * GLA_DK, (hd + 1) * GLA_DK)
        gq_ref[hd] = gq[:, sl]
        gk_ref[hd] = gk[:, sl]
        la_ref[hd] = la[:, sl]
    for c in range(sqt_ref.shape[0]):
        sqt_ref[c] = sqt[:, c * SB_TILE:(c + 1) * SB_TILE]
        svt_ref[c] = svt[:, c * SB_TILE:(c + 1) * SB_TILE]


def _const_spec(shape):
    nd = len(shape)
    return pl.BlockSpec(shape, lambda *_: (0,) * nd, pipeline_mode=pl.Buffered(1))


def _ffn_proj(x, g1, wg, wu, wd, g2, wp, wup, bgk, sqg, skg, ones_bd, tm):
    B, T, D = x.shape
    nt = T // tm
    n_sub = tm // SB_TILE
    nblk = T // SB_TILE
    tok = lambda w: pl.BlockSpec((None, tm, w), lambda b, i: (b, i, 0))
    headmajor = pl.BlockSpec((None, GLA_HEADS, tm, GLA_DK), lambda b, i: (b, 0, i, 0))
    transposed = pl.BlockSpec((None, n_sub, SB_W, SB_TILE), lambda b, i: (b, i, 0, 0))
    consts = (g1, wg, wu, wd, g2, wp, wup, bgk, sqg, skg, ones_bd)
    return pl.pallas_call(
        _ffn_proj_kernel,
        grid=(B, nt),
        in_specs=[tok(D)] + [_const_spec(c.shape) for c in consts],
        out_specs=[tok(D), headmajor, headmajor, headmajor, tok(GLA_V), tok(GLA_V),
                   transposed, tok(SB_W), transposed],
        out_shape=[
            jax.ShapeDtypeStruct((B, T, D), F32),
            jax.ShapeDtypeStruct((B, GLA_HEADS, T, GLA_DK), F32),
            jax.ShapeDtypeStruct((B, GLA_HEADS, T, GLA_DK), F32),
            jax.ShapeDtypeStruct((B, GLA_HEADS, T, GLA_DK), F32),
            jax.ShapeDtypeStruct((B, T, GLA_V), BF16),
            jax.ShapeDtypeStruct((B, T, GLA_V), F32),
            jax.ShapeDtypeStruct((B, nblk, SB_W, SB_TILE), BF16),
            jax.ShapeDtypeStruct((B, T, SB_W), BF16),
            jax.ShapeDtypeStruct((B, nblk, SB_W, SB_TILE), BF16),
        ],
        compiler_params=pltpu.CompilerParams(
            dimension_semantics=("parallel", "parallel"), vmem_limit_bytes=VMEM_LIMIT_BYTES),
        name="ffn1_proj",
    )(x, *consts)


def _gla_kernel(q_ref, k_ref, la_ref, v_ref, gate_ref, g_ref, tri_ref, o_ref, state_scr):
    C, H, G = GLA_CHUNK, GLA_HEADS, GLA_GROUP
    n_steps = q_ref.shape[1] // (C * G)
    row = lax.broadcasted_iota(jnp.int32, (C, C), 0)
    col = lax.broadcasted_iota(jnp.int32, (C, C), 1)
    causal = col <= row
    tri2 = tri_ref[...]
    gain = g_ref[...]

    @pl.when(pl.program_id(1) == 0)
    def _():
        state_scr[...] = jnp.zeros(state_scr.shape, F32)

    def step(n, _):
        pairs = [(h, c) for h in range(H) for c in range(G)]
        rows = {c: pl.ds(pl.multiple_of((n * G + c) * C, C), C) for c in range(G)}
        b = {}
        for h, c in pairs:
            hi, lo = _split_bf16(la_ref[h, rows[c], :])
            b[h, c] = _dot(tri2, jnp.concatenate([hi, lo], axis=0))
        q_dec, k_inv, k_end, decay, v = {}, {}, {}, {}, {}
        for h, c in pairs:
            b_last = b[h, c][C - 1:C, :]
            k = k_ref[h, rows[c], :]
            q_dec[h, c] = (q_ref[h, rows[c], :] * jnp.exp(b[h, c])).astype(BF16)
            k_inv[h, c] = (k * jnp.exp(-b[h, c])).astype(BF16)
            k_end[h, c] = (k * jnp.exp(b_last - b[h, c])).astype(BF16)
            decay[h, c] = jnp.exp(b_last)
            v[h, c] = v_ref[rows[c], h * GLA_DV:(h + 1) * GLA_DV]
        scores = {p: _dot_nt(q_dec[p], k_inv[p]) for p in pairs}
        kv_t = {p: _dot_tn(v[p], k_end[p]) for p in pairs}
        state_in = {}
        for h in range(H):
            st = state_scr[h]
            for c in range(G):
                state_in[h, c] = st.astype(BF16)
                st = st * decay[h, c] + kv_t[h, c]
            state_scr[h] = st
        out = {}
        for p in pairs:
            sc = jnp.where(causal, scores[p], 0.0).astype(BF16)
            out[p] = _dot(sc, v[p]) + _dot_nt(q_dec[p], state_in[p])
        for h, c in pairs:
            gate = gate_ref[rows[c], h * GLA_DV:(h + 1) * GLA_DV]
            o = _rms(out[h, c], gain) * (gate * jax.nn.sigmoid(gate))
            o_ref[rows[c], h * GLA_DV:(h + 1) * GLA_DV] = o.astype(o_ref.dtype)
        return 0

    lax.fori_loop(0, n_steps, step, 0)


def _gla(gq, gk, la, gv, gg, gain, tri2):
    B, H, T, DK = gq.shape
    tt = min(GLA_TIME_TILE, T)
    hm = pl.BlockSpec((None, H, tt, DK), lambda b, t: (b, 0, t, 0))
    tok = pl.BlockSpec((None, tt, GLA_V), lambda b, t: (b, t, 0))
    return pl.pallas_call(
        _gla_kernel,
        grid=(B, T // tt),
        in_specs=[hm, hm, hm, tok, tok, _const_spec(gain.shape), _const_spec(tri2.shape)],
        out_specs=tok,
        out_shape=jax.ShapeDtypeStruct((B, T, GLA_V), BF16),
        scratch_shapes=[pltpu.VMEM((H, GLA_DV, DK), F32)],
        compiler_params=pltpu.CompilerParams(
            dimension_semantics=("parallel", "arbitrary"), vmem_limit_bytes=VMEM_LIMIT_BYTES),
        name="gla",
    )(gq, gk, la, gv, gg, gain, tri2)


def _sb_kernel(diag_ref, off_ref, qt_ref, k_ref, vt_ref, tri_ref, o_ref,
               qh_scr, z_scr, n_scr, acc_scr, rest_scr):
    S, K = SB_TILE, SB_KEYS
    n_sub = S // K
    nblk = qt_ref.shape[0]
    neg_tri2 = tri_ref[...]

    row = lax.broadcasted_iota(jnp.int32, (2 * SB_DH, S), 0)
    for blk in range(nblk):
        qt = qt_ref[blk]
        qh_scr[blk, 0] = jnp.where(row < SB_DH, qt, jnp.zeros_like(qt))
        qh_scr[blk, 1] = jnp.where(row >= SB_DH, qt, jnp.zeros_like(qt))
    z_scr[...] = jnp.zeros(z_scr.shape, F32)
    n_scr[...] = jnp.zeros(n_scr.shape, F32)

    def sweep(task_ref, n_tasks, diagonal):
        def step(u, phase, stage_a=True, stage_b=True):
            slot_a, slot_b, slot_c = phase % SB_STAGES, (phase + 2) % SB_STAGES, (phase + 1) % SB_STAGES

            i_a = task_ref[0, u]
            j_a = task_ref[1, u]
            for sub in range(n_sub if stage_a else 0):
                kb = k_ref[pl.ds(pl.multiple_of(j_a * S + sub * K, K), K), :]
                if diagonal:
                    keep = (lax.broadcasted_iota(jnp.int32, (K, S), 0) + sub * K
                            < lax.broadcasted_iota(jnp.int32, (K, S), 1))
                for hd in range(2):
                    z = _dot(kb, qh_scr[i_a, hd])
                    if diagonal:
                        z = jnp.where(keep, z, SB_MASKED)
                    z_scr[slot_a, sub, hd] = z

            for sub in range(n_sub if stage_b else 0):
                for hd in range(2):
                    parts = []
                    for r0 in range(0, K, SB_ROWS):
                        hi, lo = _split_bf16(_softplus_log2(z_scr[slot_b, sub, hd, r0:r0 + SB_ROWS, :]))
                        parts += [hi, lo]
                    n_scr[slot_b, sub, hd] = _dot(neg_tri2, jnp.concatenate(parts, axis=0))

            i_c = task_ref[3, u]
            vt = vt_ref[task_ref[2, u]]
            for hd in range(2):
                if diagonal:
                    acc = jnp.zeros((SB_DH, S), F32)
                    rest = jnp.zeros((1, S), F32)
                else:
                    acc = acc_scr[i_c, hd]
                    rest = rest_scr[i_c, hd]
                for sub in reversed(range(n_sub)):
                    ncum = n_scr[slot_c, sub, hd]
                    w = jnp.exp2(z_scr[slot_c, sub, hd] + ncum)
                    v_t = vt[hd * SB_DH:(hd + 1) * SB_DH, sub * K:(sub + 1) * K]
                    acc = acc + _dot(v_t, w.astype(BF16)) * jnp.exp2(rest)
                    rest = rest + ncum[0:1, :]
                acc_scr[i_c, hd] = acc
                rest_scr[i_c, hd] = rest

        def body(it, _):
            for phase in range(SB_BODY_STEPS):
                step(it * SB_BODY_STEPS + phase, phase)
            return 0

        n_steps = _sb_steps(n_tasks)
        n_bodies = n_steps // SB_BODY_STEPS
        if n_bodies:
            lax.fori_loop(0, n_bodies, body, 0)
        for u in range(n_bodies * SB_BODY_STEPS, n_steps):
            step(u, u % SB_STAGES, stage_a=u < n_tasks, stage_b=u - 1 < n_tasks)

    sweep(diag_ref, nblk, True)
    if nblk > 1:
        sweep(off_ref, nblk * (nblk - 1) // 2, False)
    for blk in range(nblk):
        o_ref[blk * S:(blk + 1) * S, :] = jnp.concatenate(
            [acc_scr[blk, 0].T, acc_scr[blk, 1].T], axis=1).astype(o_ref.dtype)


def _sb_steps(n_tasks):
    return n_tasks + SB_STAGES - 1


def _sb_task_tables(nblk):
    def table(tasks):
        steps = _sb_steps(len(tasks))
        pad = [tasks[-1]] * (steps - len(tasks))
        stage_a = tasks + pad
        stage_c = [(nblk, tasks[0][1])] * 2 + tasks + [(nblk, tasks[-1][1])] * (steps - len(tasks) - 2)
        rows = [[t[0] for t in stage_a], [t[1] for t in stage_a],
                [t[1] for t in stage_c], [t[0] for t in stage_c]]
        return jnp.asarray(np.array(rows, dtype=np.int32))
    diag = [(i, i) for i in range(nblk)]
    off = [(i, j) for i in range(1, nblk) for j in range(i - 1, -1, -1)]
    return table(diag), table(off if off else diag)


def _stick_breaking(sqt, sk, svt, neg_tri2):
    B, T, W = sk.shape
    S, K = SB_TILE, SB_KEYS
    nblk = T // S
    pairs = SB_HEADS // 2
    diag_tab, off_tab = _sb_task_tables(nblk)
    grid_spec = pltpu.PrefetchScalarGridSpec(
        num_scalar_prefetch=2,
        grid=(B, pairs),
        in_specs=[
            pl.BlockSpec((None, nblk, 2 * SB_DH, S), lambda b, p, *_: (b, 0, p, 0)),
            pl.BlockSpec((None, T, 2 * SB_DH), lambda b, p, *_: (b, 0, p)),
            pl.BlockSpec((None, nblk, 2 * SB_DH, S), lambda b, p, *_: (b, 0, p, 0)),
            pl.BlockSpec(neg_tri2.shape, lambda b, p, *_: (0, 0)),
        ],
        out_specs=pl.BlockSpec((None, T, 2 * SB_DH), lambda b, p, *_: (b, 0, p)),
        scratch_shapes=[
            pltpu.VMEM((nblk, 2, 2 * SB_DH, S), BF16),
            pltpu.VMEM((SB_STAGES, S // K, 2, K, S), F32),
            pltpu.VMEM((SB_STAGES, S // K, 2, K, S), F32),
            pltpu.VMEM((nblk + 1, 2, SB_DH, S), F32),
            pltpu.VMEM((nblk + 1, 2, 1, S), F32),
        ],
    )
    return pl.pallas_call(
        _sb_kernel,
        grid_spec=grid_spec,
        out_shape=jax.ShapeDtypeStruct((B, T, W), BF16),
        compiler_params=pltpu.CompilerParams(
            dimension_semantics=("parallel", "parallel"), vmem_limit_bytes=VMEM_LIMIT_BYTES),
        name="stick_breaking",
    )(diag_tab, off_tab, sqt, sk, svt, neg_tri2)


def _out_ffn_kernel(x1_ref, og_ref, os_ref, wo_ref, g_ref, wg_ref, wu_ref, wd_ref, y_ref):
    x2 = (x1_ref[...] + _dot(og_ref[...], wo_ref[:GLA_V, :]) + _dot(os_ref[...], wo_ref[GLA_V:, :]))
    y_ref[...] = _swiglu_half_step(x2, g_ref, wg_ref, wu_ref, wd_ref)


def _out_ffn(x1, og, osb, wo, g, wg, wu, wd, tm):
    B, T, D = x1.shape
    tok = lambda w: pl.BlockSpec((None, tm, w), lambda b, i: (b, i, 0))
    consts = (wo, g, wg, wu, wd)
    return pl.pallas_call(
        _out_ffn_kernel,
        grid=(B, T // tm),
        in_specs=[tok(D), tok(GLA_V), tok(SB_W)] + [_const_spec(c.shape) for c in consts],
        out_specs=tok(D),
        out_shape=jax.ShapeDtypeStruct((B, T, D), F32),
        compiler_params=pltpu.CompilerParams(
            dimension_semantics=("parallel", "parallel"), vmem_limit_bytes=VMEM_LIMIT_BYTES),
        name="out_ffn2",
    )(x1, og, osb, *consts)


def _tri_pair(n, lower, value):
    r = np.arange(n)[:, None]
    c = np.arange(n)[None, :]
    m = np.where((c <= r) if lower else (c >= r), value, 0.0).astype(np.float32)
    return jnp.asarray(np.concatenate([m, m], axis=1), dtype=BF16)


def _tri_interleaved(n, group, value):
    r = np.arange(n)[:, None]
    c = np.arange(n)[None, :]
    m = np.where(c >= r, value, 0.0).astype(np.float32).reshape(n, n // group, 1, group)
    return jnp.asarray(np.broadcast_to(m, (n, n // group, 2, group)).reshape(n, 2 * n), dtype=BF16)


def kernel(x, ffn1_norm, ffn1_w_gate, ffn1_w_up, ffn1_w_down, mix_norm, w_in, w_gk_up, b_gk,
           gla_out_norm, sb_q_norm, sb_k_norm, w_out, ffn2_norm, ffn2_w_gate, ffn2_w_up, ffn2_w_down):
    B, T, D = x.shape
    tm = min(TOKEN_TILE, T)
    assert T % tm == 0 and tm % SB_TILE == 0 and T % (GLA_CHUNK * GLA_GROUP) == 0
    depth = w_in.shape[0]
    ones_bd = jnp.asarray(np.kron(np.eye(SB_HEADS), np.ones((SB_DH, SB_DH))), dtype=BF16)
    gla_tri2 = _tri_pair(GLA_CHUNK, True, 1.0)
    sb_neg_tri2 = _tri_interleaved(SB_KEYS, SB_ROWS, -1.0)
    row = lambda v: v.reshape(1, -1).astype(F32)

    for l in range(depth):
        g_q, g_k, g_v, g_gate, g_lr, s_q, s_k, s_v = jnp.split(
            w_in[l], np.cumsum((GLA_QK, GLA_QK, GLA_V, GLA_V, GLA_RANK, SB_W, SB_W))[:].tolist(), axis=1)
        lr_pad = jnp.pad(g_lr, ((0, 0), (0, RANK_PAD - GLA_RANK)))
        wp = jnp.concatenate([g_q, g_k, g_v, g_gate, s_q, s_k, s_v, lr_pad], axis=1).astype(BF16)
        wup = jnp.pad(w_gk_up[l], ((0, RANK_PAD - GLA_RANK), (0, 0))).astype(BF16)

        x1, gq, gk, la, gv, gg, sqt, sk, svt = _ffn_proj(
            x, row(ffn1_norm[l]), ffn1_w_gate[l].astype(BF16), ffn1_w_up[l].astype(BF16),
            ffn1_w_down[l].astype(BF16), row(mix_norm[l]), wp, wup, row(b_gk[l]),
            row(jnp.tile(sb_q_norm[l], SB_HEADS)), row(jnp.tile(sb_k_norm[l], SB_HEADS)), ones_bd, tm)
        o_gla = _gla(gq, gk, la, gv, gg, row(gla_out_norm[l]), gla_tri2)
        o_sb = _stick_breaking(sqt, sk, svt, sb_neg_tri2)
        x = _out_ffn(x1, o_gla, o_sb, w_out[l].astype(BF16), row(ffn2_norm[l]),
                     ffn2_w_gate[l].astype(BF16), ffn2_w_up[l].astype(BF16),
                     ffn2_w_down[l].astype(BF16), tm)
    return x
```

```python
import jax
import jax.numpy as jnp
import numpy as np
from jax import lax
from jax.experimental import pallas as pl
from jax.experimental.pallas import tpu as pltpu

EPS = 1e-6
GLA_HEADS = 4
GLA_DK = 64
GLA_DV = 128
GLA_RANK = 16
GLA_GATE_NORM = 16.0
GLA_CHUNK = 64
GLA_GROUP = 4
GLA_TIME_TILE = 1024
SB_HEADS = 8
SB_DH = 64
GLA_QK = GLA_HEADS * GLA_DK
GLA_V = GLA_HEADS * GLA_DV
SB_W = SB_HEADS * SB_DH

LANES = 128
MXU_TILE = 256
RANK_PAD = LANES
VMEM_LIMIT_BYTES = 56 * 1024 * 1024

TOKEN_TILE = 512
SB_TILE = 256
SB_KEYS = 128
SB_STAGES = 3
SB_BODY_STEPS = 30
SB_ROWS = 16
SB_PAD_ROWS = 8
SB_MASKED = -1e30
FFN_CHUNKS = 2

LOG2E = 1.4426950408889634
SOFTPLUS_CLAMP = 64.0
F32 = jnp.float32
BF16 = jnp.bfloat16

_C_GQ = 0
_C_GK = _C_GQ + GLA_QK
_C_GV = _C_GK + GLA_QK
_C_GG = _C_GV + GLA_V
_C_SQ = _C_GG + GLA_V
_C_SK = _C_SQ + SB_W
_C_SV = _C_SK + SB_W
_C_LR = _C_SV + SB_W
_C_END = _C_LR + RANK_PAD


def _dot(a, b):
    return jnp.dot(a, b, preferred_element_type=F32)


def _dot_nt(a, b):
    return lax.dot_general(a, b, (((1,), (1,)), ((), ())), preferred_element_type=F32)


def _dot_tn(a, b):
    return lax.dot_general(a, b, (((0,), (0,)), ((), ())), preferred_element_type=F32)


def _rms(x, g):
    return x * lax.rsqrt(jnp.mean(x * x, axis=-1, keepdims=True) + EPS) * g


def _softplus_log2(z2):
    return jnp.maximum(z2, jnp.log(1.0 + jnp.exp2(jnp.minimum(z2, SOFTPLUS_CLAMP))) * LOG2E)


def _softplus(z):
    return jnp.maximum(z, 0.0) + jnp.log(1.0 + jnp.exp2(jnp.abs(z) * (-LOG2E)))


def _split_bf16(x):
    hi = x.astype(BF16)
    lo = (x - hi.astype(F32)).astype(BF16)
    return hi, lo


def _ffn_chunks(d_ff):
    tiles = -(-d_ff // MXU_TILE)
    edges = [min(d_ff, -(-tiles * c // FFN_CHUNKS) * MXU_TILE) for c in range(FFN_CHUNKS + 1)]
    return list(zip(edges[:-1], edges[1:]))


def _swiglu_half_step(x, g_ref, wg_ref, wu_ref, wd_ref):
    h = _rms(x, g_ref[...]).astype(BF16)
    acc = jnp.zeros(x.shape, F32)
    for lo, hi in _ffn_chunks(wg_ref.shape[1]):
        gate = _dot(h, wg_ref[:, lo:hi])
        up = _dot(h, wu_ref[:, lo:hi])
        act = (gate * jax.nn.sigmoid(gate) * up).astype(BF16)
        acc = acc + _dot(act, wd_ref[lo:hi, :])
    return x + 0.5 * acc


def _ffn_proj_kernel(x_ref, g1_ref, wg_ref, wu_ref, wd_ref, g2_ref, wp_ref, wup_ref, bgk_ref,
                     sqg_ref, skg_ref, ones_ref,
                     x1_ref, gq_ref, gk_ref, la_ref, gv_ref, gg_ref, sqt_ref, sk_ref, svt_ref):
    x1 = _swiglu_half_step(x_ref[...], g1_ref, wg_ref, wu_ref, wd_ref)
    x1_ref[...] = x1
    h = _rms(x1, g2_ref[...]).astype(BF16)

    def proj(lo, hi):
        return _dot(h, wp_ref[:, lo:hi])

    lr = proj(_C_LR, _C_END).astype(BF16)
    sq = proj(_C_SQ, _C_SK)
    sk = proj(_C_SK, _C_SV)
    svt = proj(_C_SV, _C_LR).T.astype(BF16)
    gq = proj(_C_GQ, _C_GK) * (GLA_DK ** -0.5)
    gk = proj(_C_GK, _C_GV)

    def head_norm(p, g_ref):
        sq2 = (p * p).astype(BF16)
        ms = jnp.concatenate(
            [_dot(sq2[:, c:c + MXU_TILE], ones_ref[c:c + MXU_TILE, c:c + MXU_TILE])
             for c in range(0, SB_W, MXU_TILE)], axis=1) * (1.0 / SB_DH)
        return p * lax.rsqrt(ms + EPS) * g_ref[...]

    pre = _dot(lr, wup_ref[...]) + bgk_ref[...]
    sqt = (head_norm(sq, sqg_ref) * (SB_DH ** -0.5 * LOG2E)).T.astype(BF16)
    sk_ref[...] = head_norm(sk, skg_ref).astype(BF16)
    gv_ref[...] = proj(_C_GV, _C_GG).astype(BF16)
    gg_ref[...] = proj(_C_GG, _C_SQ)

    la = -_softplus(-pre) * (1.0 / GLA_GATE_NORM)
    for hd in range(GLA_HEADS):
        sl = slice(hd * GLA_DK, (hd + 1) * GLA_DK)
        gq_ref[hd] = gq[:, sl]
        gk_ref[hd] = gk[:, sl]
        la_ref[hd] = la[:, sl]
    for c in range(sqt_ref.shape[0]):
        sqt_ref[c] = sqt[:, c * SB_TILE:(c + 1) * SB_TILE]
        svt_ref[c] = svt[:, c * SB_TILE:(c + 1) * SB_TILE]


def _const_spec(shape):
    nd = len(shape)
    return pl.BlockSpec(shape, lambda *_: (0,) * nd, pipeline_mode=pl.Buffered(1))


def _ffn_proj(x, g1, wg, wu, wd, g2, wp, wup, bgk, sqg, skg, ones_bd, tm):
    B, T, D = x.shape
    nt = T // tm
    n_sub = tm // SB_TILE
    nblk = T // SB_TILE
    tok = lambda w: pl.BlockSpec((None, tm, w), lambda b, i: (b, i, 0))
    headmajor = pl.BlockSpec((None, GLA_HEADS, tm, GLA_DK), lambda b, i: (b, 0, i, 0))
    transposed = pl.BlockSpec((None, n_sub, SB_W, SB_TILE), lambda b, i: (b, i, 0, 0))
    consts = (g1, wg, wu, wd, g2, wp, wup, bgk, sqg, skg, ones_bd)
    return pl.pallas_call(
        _ffn_proj_kernel,
        grid=(B, nt),
        in_specs=[tok(D)] + [_const_spec(c.shape) for c in consts],
        out_specs=[tok(D), headmajor, headmajor, headmajor, tok(GLA_V), tok(GLA_V),
                   transposed, tok(SB_W), transposed],
        out_shape=[
            jax.ShapeDtypeStruct((B, T, D), F32),
            jax.ShapeDtypeStruct((B, GLA_HEADS, T, GLA_DK), F32),
            jax.ShapeDtypeStruct((B, GLA_HEADS, T, GLA_DK), F32),
            jax.ShapeDtypeStruct((B, GLA_HEADS, T, GLA_DK), F32),
            jax.ShapeDtypeStruct((B, T, GLA_V), BF16),
            jax.ShapeDtypeStruct((B, T, GLA_V), F32),
            jax.ShapeDtypeStruct((B, nblk, SB_W, SB_TILE), BF16),
            jax.ShapeDtypeStruct((B, T, SB_W), BF16),
            jax.ShapeDtypeStruct((B, nblk, SB_W, SB_TILE), BF16),
        ],
        compiler_params=pltpu.CompilerParams(
            dimension_semantics=("parallel", "parallel"), vmem_limit_bytes=VMEM_LIMIT_BYTES),
        name="ffn1_proj",
    )(x, *consts)


def _gla_kernel(q_ref, k_ref, la_ref, v_ref, gate_ref, g_ref, tri_ref, o_ref, state_scr):
    C, H, G = GLA_CHUNK, GLA_HEADS, GLA_GROUP
    n_steps = q_ref.shape[1] // (C * G)
    row = lax.broadcasted_iota(jnp.int32, (C, C), 0)
    col = lax.broadcasted_iota(jnp.int32, (C, C), 1)
    causal = col <= row
    tri2 = tri_ref[...]
    gain = g_ref[...]

    @pl.when(pl.program_id(1) == 0)
    def _():
        state_scr[...] = jnp.zeros(state_scr.shape, F32)

    def step(n, _):
        pairs = [(h, c) for h in range(H) for c in range(G)]
        rows = {c: pl.ds(pl.multiple_of((n * G + c) * C, C), C) for c in range(G)}
        b = {}
        for h, c in pairs:
            hi, lo = _split_bf16(la_ref[h, rows[c], :])
            b[h, c] = _dot(tri2, jnp.concatenate([hi, lo], axis=0))
        q_dec, k_inv, k_end, decay, v = {}, {}, {}, {}, {}
        for h, c in pairs:
            b_last = b[h, c][C - 1:C, :]
            k = k_ref[h, rows[c], :]
            q_dec[h, c] = (q_ref[h, rows[c], :] * jnp.exp(b[h, c])).astype(BF16)
            k_inv[h, c] = (k * jnp.exp(-b[h, c])).astype(BF16)
            k_end[h, c] = (k * jnp.exp(b_last - b[h, c])).astype(BF16)
            decay[h, c] = jnp.exp(b_last)
            v[h, c] = v_ref[rows[c], h * GLA_DV:(h + 1) * GLA_DV]
        scores = {p: _dot_nt(q_dec[p], k_inv[p]) for p in pairs}
        kv_t = {p: _dot_tn(v[p], k_end[p]) for p in pairs}
        state_in = {}
        for h in range(H):
            st = state_scr[h]
            for c in range(G):
                state_in[h, c] = st.astype(BF16)
                st = st * decay[h, c] + kv_t[h, c]
            state_scr[h] = st
        out = {}
        for p in pairs:
            sc = jnp.where(causal, scores[p], 0.0).astype(BF16)
            out[p] = _dot(sc, v[p]) + _dot_nt(q_dec[p], state_in[p])
        for h, c in pairs:
            gate = gate_ref[rows[c], h * GLA_DV:(h + 1) * GLA_DV]
            o = _rms(out[h, c], gain) * (gate * jax.nn.sigmoid(gate))
            o_ref[rows[c], h * GLA_DV:(h + 1) * GLA_DV] = o.astype(o_ref.dtype)
        return 0

    lax.fori_loop(0, n_steps, step, 0)


def _gla(gq, gk, la, gv, gg, gain, tri2):
    B, H, T, DK = gq.shape
    tt = min(GLA_TIME_TILE, T)
    hm = pl.BlockSpec((None, H, tt, DK), lambda b, t: (b, 0, t, 0))
    tok = pl.BlockSpec((None, tt, GLA_V), lambda b, t: (b, t, 0))
    return pl.pallas_call(
        _gla_kernel,
        grid=(B, T // tt),
        in_specs=[hm, hm, hm, tok, tok, _const_spec(gain.shape), _const_spec(tri2.shape)],
        out_specs=tok,
        out_shape=jax.ShapeDtypeStruct((B, T, GLA_V), BF16),
        scratch_shapes=[pltpu.VMEM((H, GLA_DV, DK), F32)],
        compiler_params=pltpu.CompilerParams(
            dimension_semantics=("parallel", "arbitrary"), vmem_limit_bytes=VMEM_LIMIT_BYTES),
        name="gla",
    )(gq, gk, la, gv, gg, gain, tri2)


def _sb_kernel(diag_ref, off_ref, qt_ref, k_ref, vt_ref, tri_ref, o_ref,
               qh_scr, z_scr, n_scr, acc_scr, rest_scr):
    S, K = SB_TILE, SB_KEYS
    n_sub = S // K
    nblk = qt_ref.shape[0]
    neg_tri2 = tri_ref[...]

    row = lax.broadcasted_iota(jnp.int32, (2 * SB_DH, S), 0)
    for blk in range(nblk):
        qt = qt_ref[blk]
        qh_scr[blk, 0] = jnp.where(row < SB_DH, qt, jnp.zeros_like(qt))
        qh_scr[blk, 1] = jnp.where(row >= SB_DH, qt, jnp.zeros_like(qt))
    z_scr[...] = jnp.zeros(z_scr.shape, F32)
    n_scr[...] = jnp.zeros(n_scr.shape, F32)

    def sweep(task_ref, n_tasks, diagonal):
        def step(u, phase, stage_a=True, stage_b=True):
            slot_a, slot_b, slot_c = phase % SB_STAGES, (phase + 2) % SB_STAGES, (phase + 1) % SB_STAGES

            i_a = task_ref[0, u]
            j_a = task_ref[1, u]
            for sub in range(n_sub if stage_a else 0):
                kb = k_ref[pl.ds(pl.multiple_of(j_a * S + sub * K, K), K), :]
                if diagonal:
                    keep = (lax.broadcasted_iota(jnp.int32, (K, S), 0) + sub * K
                            < lax.broadcasted_iota(jnp.int32, (K, S), 1))
                for hd in range(2):
                    z = _dot(kb, qh_scr[i_a, hd])
                    if diagonal:
                        z = jnp.where(keep, z, SB_MASKED)
                    z_scr[slot_a, sub, hd] = z

            for sub in range(n_sub if stage_b else 0):
                for hd in range(2):
                    parts = []
                    for r0 in range(0, K, SB_ROWS):
                        hi, lo = _split_bf16(_softplus_log2(z_scr[slot_b, sub, hd, r0:r0 + SB_ROWS, :]))
                        parts += [hi, lo]
                    n_scr[slot_b, sub, hd, :K, :] = _dot(neg_tri2, jnp.concatenate(parts, axis=0))

            i_c = task_ref[3, u]
            vt = vt_ref[task_ref[2, u]]
            for hd in range(2):
                if diagonal:
                    acc = jnp.zeros((SB_DH, S), F32)
                    rest = jnp.zeros((1, S), F32)
                else:
                    acc = acc_scr[i_c, hd]
                    rest = rest_scr[i_c, hd]
                for sub in reversed(range(n_sub)):
                    ncum = n_scr[slot_c, sub, hd, :K, :]
                    w = jnp.exp2(z_scr[slot_c, sub, hd] + ncum)
                    v_t = vt[hd * SB_DH:(hd + 1) * SB_DH, sub * K:(sub + 1) * K]
                    acc = acc + _dot(v_t, w.astype(BF16)) * jnp.exp2(rest)
                    rest = rest + ncum[0:1, :]
                acc_scr[i_c, hd] = acc
                rest_scr[i_c, hd] = rest

        def body(it, _):
            for phase in range(SB_BODY_STEPS):
                step(it * SB_BODY_STEPS + phase, phase)
            return 0

        n_steps = _sb_steps(n_tasks)
        n_bodies = n_steps // SB_BODY_STEPS
        if n_bodies:
            lax.fori_loop(0, n_bodies, body, 0)
        for u in range(n_bodies * SB_BODY_STEPS, n_steps):
            step(u, u % SB_STAGES, stage_a=u < n_tasks, stage_b=u - 1 < n_tasks)

    sweep(diag_ref, nblk, True)
    if nblk > 1:
        sweep(off_ref, nblk * (nblk - 1) // 2, False)
    for blk in range(nblk):
        o_ref[blk * S:(blk + 1) * S, :] = jnp.concatenate(
            [acc_scr[blk, 0].T, acc_scr[blk, 1].T], axis=1).astype(o_ref.dtype)


def _sb_steps(n_tasks):
    return n_tasks + SB_STAGES - 1


def _sb_task_tables(nblk):
    def table(tasks):
        steps = _sb_steps(len(tasks))
        pad = [tasks[-1]] * (steps - len(tasks))
        stage_a = tasks + pad
        stage_c = [(nblk, tasks[0][1])] * 2 + tasks + [(nblk, tasks[-1][1])] * (steps - len(tasks) - 2)
        rows = [[t[0] for t in stage_a], [t[1] for t in stage_a],
                [t[1] for t in stage_c], [t[0] for t in stage_c]]
        return jnp.asarray(np.array(rows, dtype=np.int32))
    diag = [(i, i) for i in range(nblk)]
    off = [(i, j) for i in range(1, nblk) for j in range(i - 1, -1, -1)]
    return table(diag), table(off if off else diag)


def _stick_breaking(sqt, sk, svt, neg_tri2):
    B, T, W = sk.shape
    S, K = SB_TILE, SB_KEYS
    nblk = T // S
    pairs = SB_HEADS // 2
    diag_tab, off_tab = _sb_task_tables(nblk)
    grid_spec = pltpu.PrefetchScalarGridSpec(
        num_scalar_prefetch=2,
        grid=(B, pairs),
        in_specs=[
            pl.BlockSpec((None, nblk, 2 * SB_DH, S), lambda b, p, *_: (b, 0, p, 0)),
            pl.BlockSpec((None, T, 2 * SB_DH), lambda b, p, *_: (b, 0, p)),
            pl.BlockSpec((None, nblk, 2 * SB_DH, S), lambda b, p, *_: (b, 0, p, 0)),
            pl.BlockSpec(neg_tri2.shape, lambda b, p, *_: (0, 0)),
        ],
        out_specs=pl.BlockSpec((None, T, 2 * SB_DH), lambda b, p, *_: (b, 0, p)),
        scratch_shapes=[
            pltpu.VMEM((nblk, 2, 2 * SB_DH, S), BF16),
            pltpu.VMEM((SB_STAGES, S // K, 2, K, S), F32),
            pltpu.VMEM((SB_STAGES, S // K, 2, K + SB_PAD_ROWS, S), F32),
            pltpu.VMEM((nblk + 1, 2, SB_DH, S), F32),
            pltpu.VMEM((nblk + 1, 2, 1, S), F32),
        ],
    )
    return pl.pallas_call(
        _sb_kernel,
        grid_spec=grid_spec,
        out_shape=jax.ShapeDtypeStruct((B, T, W), BF16),
        compiler_params=pltpu.CompilerParams(
            dimension_semantics=("parallel", "parallel"), vmem_limit_bytes=VMEM_LIMIT_BYTES),
        name="stick_breaking",
    )(diag_tab, off_tab, sqt, sk, svt, neg_tri2)


def _out_ffn_kernel(x1_ref, og_ref, os_ref, wo_ref, g_ref, wg_ref, wu_ref, wd_ref, y_ref):
    x2 = (x1_ref[...] + _dot(og_ref[...], wo_ref[:GLA_V, :]) + _dot(os_ref[...], wo_ref[GLA_V:, :]))
    y_ref[...] = _swiglu_half_step(x2, g_ref, wg_ref, wu_ref, wd_ref)


def _out_ffn(x1, og, osb, wo, g, wg, wu, wd, tm):
    B, T, D = x1.shape
    tok = lambda w: pl.BlockSpec((None, tm, w), lambda b, i: (b, i, 0))
    consts = (wo, g, wg, wu, wd)
    return pl.pallas_call(
        _out_ffn_kernel,
        grid=(B, T // tm),
        in_specs=[tok(D), tok(GLA_V), tok(SB_W)] + [_const_spec(c.shape) for c in consts],
        out_specs=tok(D),
        out_shape=jax.ShapeDtypeStruct((B, T, D), F32),
        compiler_params=pltpu.CompilerParams(
            dimension_semantics=("parallel", "parallel"), vmem_limit_bytes=VMEM_LIMIT_BYTES),
        name="out_ffn2",
    )(x1, og, osb, *consts)


def _tri_pair(n, lower, value):
    r = np.arange(n)[:, None]
    c = np.arange(n)[None, :]
    m = np.where((c <= r) if lower else (c >= r), value, 0.0).astype(np.float32)
    return jnp.asarray(np.concatenate([m, m], axis=1), dtype=BF16)


def _tri_interleaved(n, group, value):
    r = np.arange(n)[:, None]
    c = np.arange(n)[None, :]
    m = np.where(c >= r, value, 0.0).astype(np.float32).reshape(n, n // group, 1, group)
    return jnp.asarray(np.broadcast_to(m, (n, n // group, 2, group)).reshape(n, 2 * n), dtype=BF16)


def kernel(x, ffn1_norm, ffn1_w_gate, ffn1_w_up, ffn1_w_down, mix_norm, w_in, w_gk_up, b_gk,
           gla_out_norm, sb_q_norm, sb_k_norm, w_out, ffn2_norm, ffn2_w_gate, ffn2_w_up, ffn2_w_down):
    B, T, D = x.shape
    tm = min(TOKEN_TILE, T)
    assert T % tm == 0 and tm % SB_TILE == 0 and T % (GLA_CHUNK * GLA_GROUP) == 0
    depth = w_in.shape[0]
    ones_bd = jnp.asarray(np.kron(np.eye(SB_HEADS), np.ones((SB_DH, SB_DH))), dtype=BF16)
    gla_tri2 = _tri_pair(GLA_CHUNK, True, 1.0)
    sb_neg_tri2 = _tri_interleaved(SB_KEYS, SB_ROWS, -1.0)
    row = lambda v: v.reshape(1, -1).astype(F32)

    for l in range(depth):
        g_q, g_k, g_v, g_gate, g_lr, s_q, s_k, s_v = jnp.split(
            w_in[l], np.cumsum((GLA_QK, GLA_QK, GLA_V, GLA_V, GLA_RANK, SB_W, SB_W))[:].tolist(), axis=1)
        lr_pad = jnp.pad(g_lr, ((0, 0), (0, RANK_PAD - GLA_RANK)))
        wp = jnp.concatenate([g_q, g_k, g_v, g_gate, s_q, s_k, s_v, lr_pad], axis=1).astype(BF16)
        wup = jnp.pad(w_gk_up[l], ((0, RANK_PAD - GLA_RANK), (0, 0))).astype(BF16)

        x1, gq, gk, la, gv, gg, sqt, sk, svt = _ffn_proj(
            x, row(ffn1_norm[l]), ffn1_w_gate[l].astype(BF16), ffn1_w_up[l].astype(BF16),
            ffn1_w_down[l].astype(BF16), row(mix_norm[l]), wp, wup, row(b_gk[l]),
            row(jnp.tile(sb_q_norm[l], SB_HEADS)), row(jnp.tile(sb_k_norm[l], SB_HEADS)), ones_bd, tm)
        o_gla = _gla(gq, gk, la, gv, gg, row(gla_out_norm[l]), gla_tri2)
        o_sb = _stick_breaking(sqt, sk, svt, sb_neg_tri2)
        x = _out_ffn(x1, o_gla, o_sb, w_out[l].astype(BF16), row(ffn2_norm[l]),
                     ffn2_w_gate[l].astype(BF16), ffn2_w_up[l].astype(BF16),
                     ffn2_w_down[l].astype(BF16), tm)
    return x
```
